```python
import math
import jax
import jax.numpy as jnp
from jax import lax
import numpy as np

D_MODEL = 1024
BATCH = 8
SEQ = 4096
DEPTH = 2

HEAD_DIM = 64
M_HEADS = 4
M_WIDTH = M_HEADS * HEAD_DIM
M_CONV = 4
M_CHUNK = 64
M_NORM_EPS = 1e-6
R_HEADS = 4
R_WIDTH = R_HEADS * HEAD_DIM
R_DECAY_LORA = 32
R_AAA_LORA = 32
R_GATE_LORA = 64
R_LN_EPS = 64e-5
A_HEADS = 4
A_QK_DIM = HEAD_DIM
A_V_DIM = 2 * HEAD_DIM
A_WIDTH = A_HEADS * A_V_DIM
A_BLOCK = 128
A_NORM_EPS = 1e-5

MIX_WIDTH = M_WIDTH + R_WIDTH + A_WIDTH
D_FF = 4 * D_MODEL
NORM_EPS = 1e-6

M_SEGS = (M_WIDTH, M_WIDTH, M_WIDTH, M_WIDTH, M_HEADS, M_HEADS)
R_SEGS = (R_WIDTH, R_WIDTH, R_WIDTH, R_DECAY_LORA, R_AAA_LORA, R_GATE_LORA)
A_SEGS = (A_HEADS * 2 * A_QK_DIM, A_HEADS * 2 * A_QK_DIM, A_WIDTH)
M_PROJ = sum(M_SEGS)
R_PROJ = sum(R_SEGS)
A_PROJ = sum(A_SEGS)
IN_PROJ = M_PROJ + R_PROJ + A_PROJ

kernel_name = 'hybrid_mlstm_rwkv7_diffattn'


def _split(t, sizes):
    idx = [int(i) for i in np.cumsum(sizes)[:-1]]
    return jnp.split(t, idx, axis=-1)


def _rms_norm(x, g, eps=NORM_EPS):
    xf = x.astype(jnp.float32)
    y = xf * lax.rsqrt(jnp.mean(jnp.square(xf), axis=-1, keepdims=True) + eps)
    return (y * g.astype(jnp.float32)).astype(x.dtype)


def _head_rms(h, g, eps):
    return h * lax.rsqrt(jnp.mean(jnp.square(h), axis=-1, keepdims=True) + eps) * g.astype(jnp.float32)


def _causal_dwconv(x, w, b):
    k = w.shape[0]
    y = lax.conv_general_dilated(x, w[:, None, :], window_strides=(1,), padding=[(k - 1, 0)],
                                 dimension_numbers=('NWC', 'WIO', 'NWC'),
                                 feature_group_count=x.shape[-1])
    return y + b


def _mlstm(q, k, v, o_pre, i_pre, f_pre, conv_w, conv_b, b_i, b_f, norm_g):
    B, S = q.shape[:2]
    qk = jax.nn.silu(_causal_dwconv(jnp.concatenate([q, k], axis=-1), conv_w, conv_b))
    q, k = jnp.split(qk.astype(jnp.float32), 2, axis=-1)

    def heads(t):
        return t.reshape(B, S, M_HEADS, HEAD_DIM).transpose(0, 2, 1, 3)

    q = heads(q)
    k = heads(k) * (HEAD_DIM ** -0.5)
    v = heads(v.astype(jnp.float32))
    ig = (i_pre.astype(jnp.float32) + b_i.astype(jnp.float32)).transpose(0, 2, 1)
    lf = jax.nn.log_sigmoid(f_pre.astype(jnp.float32) + b_f.astype(jnp.float32)).transpose(0, 2, 1)
    nc = S // M_CHUNK

    def chunks(t):
        t = t.reshape(B, M_HEADS, nc, M_CHUNK, *t.shape[3:])
        return jnp.moveaxis(t, 2, 0)

    tril = jnp.tril(jnp.ones((M_CHUNK, M_CHUNK), dtype=bool))

    def step(carry, xs):
        C, n, m = carry
        qc, kc, vc, igc, lfc = xs
        b = jnp.cumsum(lfc, axis=-1)
        D = jnp.where(tril, b[..., :, None] - b[..., None, :] + igc[..., None, :], -jnp.inf)
        inter = b + m[..., None]
        mt = jnp.maximum(inter, jnp.max(D, axis=-1))
        Dw = jnp.exp(D - mt[..., None])
        iw = jnp.exp(inter - mt)
        sqk = jnp.einsum('bhtd,bhsd->bhts', qc, kc) * Dw
        num = iw[..., None] * jnp.einsum('bhtd,bhde->bhte', qc, C) + jnp.einsum('bhts,bhse->bhte', sqk, vc)
        den = iw * jnp.einsum('bhtd,bhd->bht', qc, n) + jnp.sum(sqk, axis=-1)
        h = num / jnp.maximum(jnp.abs(den), jnp.exp(-mt))[..., None]
        bl = b[..., -1]
        g_end = bl[..., None] - b + igc
        m_new = jnp.maximum(bl + m, jnp.max(g_end, axis=-1))
        wk = jnp.exp(g_end - m_new[..., None])
        cs = jnp.exp(bl + m - m_new)
        C = cs[..., None, None] * C + jnp.einsum('bhs,bhsd,bhse->bhde', wk, kc, vc)
        n = cs[..., None] * n + jnp.einsum('bhs,bhsd->bhd', wk, kc)
        return (C, n, m_new), h

    init = (jnp.zeros((B, M_HEADS, HEAD_DIM, HEAD_DIM), jnp.float32),
            jnp.zeros((B, M_HEADS, HEAD_DIM), jnp.float32),
            jnp.zeros((B, M_HEADS), jnp.float32))
    _, h = lax.scan(step, init, (chunks(q), chunks(k), chunks(v), chunks(ig), chunks(lf)))
    h = jnp.moveaxis(h, 0, 2).reshape(B, M_HEADS, S, HEAD_DIM).transpose(0, 2, 1, 3)
    h = _head_rms(h, norm_g.reshape(M_HEADS, HEAD_DIM), M_NORM_EPS)
    return h.reshape(B, S, M_WIDTH) * jax.nn.sigmoid(o_pre.astype(jnp.float32))


def _rwkv7(p, mu, w0, w_up, a0, a_up, g_up, k_k, k_a, r_k, ln_g, ln_b):
    B, S = p.shape[:2]
    p = p.astype(jnp.float32)
    p_prev = jnp.pad(p, ((0, 0), (1, 0), (0, 0)))[:, :-1]
    p = p + (p_prev - p) * mu.astype(jnp.float32)
    r, k, v, wd, ad, gd = _split(p, R_SEGS)
    w_log = -jax.nn.softplus(-(w0 + jnp.tanh(wd) @ w_up.astype(jnp.float32))) - 0.5
    decay = jnp.exp(-jnp.exp(w_log))
    a = jax.nn.sigmoid(a0 + ad @ a_up.astype(jnp.float32))
    g = jax.nn.sigmoid(gd) @ g_up.astype(jnp.float32)

    def heads(t):
        return t.reshape(B, S, R_HEADS, HEAD_DIM)

    kk = heads(k * k_k)
    kk = kk / jnp.maximum(jnp.sqrt(jnp.sum(jnp.square(kk), axis=-1, keepdims=True)), 1e-12)
    k = k * (1.0 + (a - 1.0) * k_a)
    r, k, v, decay, a = heads(r), heads(k), heads(v), heads(decay), heads(a)

    def step(st, xs):
        rt, wt, kt, vt, kkt, at = xs
        sa = jnp.einsum('bhij,bhj->bhi', st, -kkt)
        st = (st * wt[:, :, None, :] + sa[..., :, None] * (kkt * at)[:, :, None, :]
              + vt[..., :, None] * kt[:, :, None, :])
        return st, jnp.einsum('bhij,bhj->bhi', st, rt)

    xs = tuple(jnp.moveaxis(t, 1, 0) for t in (r, decay, k, v, kk, a))
    _, y = lax.scan(step, jnp.zeros((B, R_HEADS, HEAD_DIM, HEAD_DIM), jnp.float32), xs)
    y = jnp.moveaxis(y, 0, 1)
    mean = jnp.mean(y, axis=-1, keepdims=True)
    var = jnp.mean(jnp.square(y - mean), axis=-1, keepdims=True)
    y = ((y - mean) * lax.rsqrt(var + R_LN_EPS) * ln_g.reshape(R_HEADS, HEAD_DIM)
         + ln_b.reshape(R_HEADS, HEAD_DIM))
    y = y + jnp.sum(r * k * r_k, axis=-1, keepdims=True) * v
    return y.reshape(B, S, R_WIDTH) * g


def _diff_attn(q, k, v, lq1, lk1, lq2, lk2, norm_g, lam_init):
    B, S = q.shape[:2]
    nb = S // A_BLOCK
    q = q.astype(jnp.float32).reshape(B, S, A_HEADS, 2, A_QK_DIM).transpose(0, 2, 3, 1, 4)
    k = k.astype(jnp.float32).reshape(B, S, A_HEADS, 2, A_QK_DIM).transpose(0, 2, 3, 1, 4)
    v = v.astype(jnp.float32).reshape(B, S, A_HEADS, A_V_DIM).transpose(0, 2, 1, 3)
    f32 = jnp.float32
    lam = (jnp.exp(jnp.sum(lq1.astype(f32) * lk1.astype(f32)))
           - jnp.exp(jnp.sum(lq2.astype(f32) * lk2.astype(f32))) + lam_init)
    q_blocks = jnp.moveaxis(q.reshape(B, A_HEADS, 2, nb, A_BLOCK, A_QK_DIM), 3, 0)
    key_pos = jnp.arange(S)
    scale = A_QK_DIM ** -0.5

    def block(args):
        qb, bi = args
        s = jnp.einsum('bhcqd,bhckd->bhcqk', qb, k) * scale
        q_pos = bi * A_BLOCK + jnp.arange(A_BLOCK)
        mask = key_pos[None, :] <= q_pos[:, None]
        pr = jax.nn.softmax(jnp.where(mask, s, -jnp.inf), axis=-1)
        attn = pr[:, :, 0] - lam * pr[:, :, 1]
        return jnp.einsum('bhqk,bhke->bhqe', attn, v)

    o = lax.map(block, (q_blocks, jnp.arange(nb)))
    o = jnp.moveaxis(o, 0, 2).reshape(B, A_HEADS, S, A_V_DIM).transpose(0, 2, 1, 3)
    o = _head_rms(o, norm_g.reshape(A_HEADS, A_V_DIM), A_NORM_EPS) * (1.0 - lam_init)
    return o.reshape(B, S, A_WIDTH)


def setup_inputs(seed: int = 0) -> dict:
    key = jax.random.key(seed)
    ks = list(jax.random.split(key, 32))
    L = DEPTH

    def nrm(i, shape, s):
        return jax.random.normal(ks[i], shape, jnp.float32) * s

    def uni(i, shape, lo, hi):
        return jax.random.uniform(ks[i], shape, jnp.float32, lo, hi)

    return {
        'x': nrm(0, (BATCH, SEQ, D_MODEL), 1.0),
        'norm1_g': 1.0 + nrm(1, (L, D_MODEL), 0.02),
        'w_in': nrm(2, (L, D_MODEL, IN_PROJ), D_MODEL ** -0.5),
        'm_conv_w': nrm(3, (L, M_CONV, 2 * M_WIDTH), M_CONV ** -0.5),
        'm_conv_b': nrm(4, (L, 2 * M_WIDTH), 0.02),
        'm_b_i': nrm(5, (L, M_HEADS), 0.1),
        'm_b_f': uni(6, (L, M_HEADS), 3.0, 6.0),
        'm_norm_g': 1.0 + nrm(7, (L, M_WIDTH), 0.02),
        'r_mu': uni(8, (L, R_PROJ), 0.0, 1.0),
        'r_w0': uni(9, (L, R_WIDTH), -6.0, -1.0),
        'r_w_up': nrm(10, (L, R_DECAY_LORA, R_WIDTH), R_DECAY_LORA ** -0.5),
        'r_a0': nrm(11, (L, R_WIDTH), 0.1),
        'r_a_up': nrm(12, (L, R_AAA_LORA, R_WIDTH), R_AAA_LORA ** -0.5),
        'r_g_up': nrm(13, (L, R_GATE_LORA, R_WIDTH), R_GATE_LORA ** -0.5),
        'r_k_k': 0.85 + nrm(14, (L, R_WIDTH), 0.02),
        'r_k_a': 1.0 + nrm(15, (L, R_WIDTH), 0.02),
        'r_r_k': nrm(16, (L, R_HEADS, HEAD_DIM), 0.1),
        'r_ln_g': 1.0 + nrm(17, (L, R_WIDTH), 0.02),
        'r_ln_b': nrm(18, (L, R_WIDTH), 0.02),
        'a_lq1': nrm(19, (L, A_QK_DIM), 0.1),
        'a_lk1': nrm(20, (L, A_QK_DIM), 0.1),
        'a_lq2': nrm(21, (L, A_QK_DIM), 0.1),
        'a_lk2': nrm(22, (L, A_QK_DIM), 0.1),
        'a_norm_g': 1.0 + nrm(23, (L, A_WIDTH), 0.02),
        'w_out': nrm(24, (L, MIX_WIDTH, D_MODEL), MIX_WIDTH ** -0.5),
        'norm2_g': 1.0 + nrm(25, (L, D_MODEL), 0.02),
        'w_ff_up': nrm(26, (L, D_MODEL, D_FF), D_MODEL ** -0.5),
        'w_ff_down': nrm(27, (L, D_FF, D_MODEL), D_FF ** -0.5),
        'final_g': 1.0 + nrm(28, (D_MODEL,), 0.02),
    }


def reference(x, norm1_g, w_in, m_conv_w, m_conv_b, m_b_i, m_b_f, m_norm_g,
              r_mu, r_w0, r_w_up, r_a0, r_a_up, r_g_up, r_k_k, r_k_a, r_r_k, r_ln_g, r_ln_b,
              a_lq1, a_lk1, a_lq2, a_lk2, a_norm_g, w_out, norm2_g, w_ff_up, w_ff_down, final_g):
    for l in range(DEPTH):
        h = _rms_norm(x, norm1_g[l])
        proj = h @ w_in[l]
        pm, pr, pa = _split(proj, (M_PROJ, R_PROJ, A_PROJ))
        mq, mk, mv, mo, mi, mf = _split(pm, M_SEGS)
        y_m = _mlstm(mq, mk, mv, mo, mi, mf, m_conv_w[l], m_conv_b[l], m_b_i[l], m_b_f[l], m_norm_g[l])
        y_r = _rwkv7(pr, r_mu[l], r_w0[l], r_w_up[l], r_a0[l], r_a_up[l], r_g_up[l],
                     r_k_k[l], r_k_a[l], r_r_k[l], r_ln_g[l], r_ln_b[l])
        aq, ak, av = _split(pa, A_SEGS)
        lam_init = 0.8 - 0.6 * math.exp(-0.3 * l)
        y_a = _diff_attn(aq, ak, av, a_lq1[l], a_lk1[l], a_lq2[l], a_lk2[l], a_norm_g[l], lam_init)
        mix = jnp.concatenate([y_m, y_r, y_a], axis=-1).astype(x.dtype)
        x = x + mix @ w_out[l]
        h = _rms_norm(x, norm2_g[l])
        x = x + jnp.square(jax.nn.relu(h @ w_ff_up[l])) @ w_ff_down[l]
    return _rms_norm(x, final_g)
```

```python
import functools
import math

import jax
import jax.numpy as jnp
from jax import lax
from jax.experimental import pallas as pl
from jax.experimental.pallas import tpu as pltpu

F32 = jnp.float32
BF16 = jnp.bfloat16
HI = lax.Precision.HIGHEST

D_MODEL = 1024
HEAD_DIM = 64
N_HEADS = 4
M_WIDTH = 256
R_WIDTH = 256
R_PROJ = 896
A_WIDTH = 512
A_PROJ = 1536
D_FF = 4096
CHUNK = 64
NORM_EPS = 1e-6
M_NORM_EPS = 1e-6
R_LN_EPS = 64e-5
A_NORM_EPS = 1e-5
MAIN_PROJ = 2048
GATE_COL_BLOCK = 7
VMEM_LIMIT = 52 * 1024 * 1024


def _dot(a, b, prec=None):
    return jnp.dot(a, b, preferred_element_type=F32, precision=prec)


def _dot_nt(a, b, prec=None):
    return lax.dot_general(a, b, (((1,), (1,)), ((), ())), preferred_element_type=F32, precision=prec)


def _dot_tn(a, b, prec=None):
    return lax.dot_general(a, b, (((0,), (0,)), ((), ())), preferred_element_type=F32, precision=prec)


def _sigmoid(x):
    return 1.0 / (1.0 + jnp.exp(-x))


def _log_sigmoid(x):
    return jnp.minimum(x, 0.0) - jnp.log(1.0 + jnp.exp(-jnp.abs(x)))


def _iota2(shape, dim):
    return lax.broadcasted_iota(jnp.int32, shape, dim)


def _norm_matmul_kernel(x_ref, g_ref, w_ref, o_ref, h_ref):
    @pl.when(pl.program_id(1) == 0)
    def _():
        x = x_ref[...]
        y = x * lax.rsqrt(jnp.mean(x * x, axis=-1, keepdims=True) + NORM_EPS) * g_ref[...]
        h_ref[...] = y.astype(BF16)

    o_ref[...] = _dot(h_ref[...], w_ref[...]).astype(o_ref.dtype)


def _norm_matmul(x, g, w, out_dtype, tm=1024, tn=512):
    t, d = x.shape
    n = w.shape[1]
    return pl.pallas_call(
        _norm_matmul_kernel,
        grid=(t // tm, n // tn),
        in_specs=[
            pl.BlockSpec((tm, d), lambda i, j: (i, 0)),
            pl.BlockSpec((1, d), lambda i, j: (0, 0)),
            pl.BlockSpec((d, tn), lambda i, j: (0, j)),
        ],
        out_specs=pl.BlockSpec((tm, tn), lambda i, j: (i, j)),
        out_shape=jax.ShapeDtypeStruct((t, n), out_dtype),
        scratch_shapes=[pltpu.VMEM((tm, d), BF16)],
        compiler_params=pltpu.CompilerParams(
            dimension_semantics=("parallel", "arbitrary"), vmem_limit_bytes=VMEM_LIMIT),
        name="norm_matmul",
    )(x, g, w)


def _mlstm_kernel(x_ref, gc_ref, gr_ref, cw_ref, cb_ref, bc_ref, br_ref, ng_ref, o_ref,
                  carry_ref, c_ref, n_ref, m_ref, *, tb):
    @pl.when(pl.program_id(1) == 0)
    def _():
        carry_ref[...] = jnp.zeros_like(carry_ref)
        c_ref[...] = jnp.zeros_like(c_ref)
        n_ref[...] = jnp.zeros_like(n_ref)
        m_ref[...] = jnp.zeros_like(m_ref)

    x = x_ref[...]
    qk_pre = x[:, :2 * M_WIDTH]
    ext = jnp.concatenate([carry_ref[...], qk_pre], axis=0)
    carry_ref[...] = qk_pre[tb - 8:, :]
    cw = cw_ref[...]
    acc = cb_ref[...] + cw[3:4, :] * qk_pre
    for j in (1, 2, 3):
        acc = acc + cw[3 - j:4 - j, :] * pltpu.roll(ext, j, axis=0)[8:, :]
    qk = acc * _sigmoid(acc)
    q = qk[:, :M_WIDTH]
    k = qk[:, M_WIDTH:] * (HEAD_DIM ** -0.5)
    v = x[:, 2 * M_WIDTH:3 * M_WIDTH]
    o_gate = _sigmoid(x[:, 3 * M_WIDTH:])

    gc = gc_ref[...] + bc_ref[...]
    gc = jnp.where(_iota2(gc.shape, 1) < N_HEADS, gc, _log_sigmoid(gc))
    gr = gr_ref[...] + br_ref[...]
    gr = jnp.where(_iota2(gr.shape, 0) < N_HEADS, gr, _log_sigmoid(gr))

    row = _iota2((CHUNK, CHUNK), 0)
    col = _iota2((CHUNK, CHUNK), 1)
    causal = col <= row
    tril = causal.astype(F32)
    triu = (row <= col).astype(F32)
    ng = ng_ref[...]

    for c in range(tb // CHUNK):
        rs = slice(c * CHUNK, (c + 1) * CHUNK)
        gcc = gc[rs, :]
        grc = gr[:, rs]
        bcol_all = _dot(tril, gcc, HI)
        brow_all = _dot(grc, triu, HI)
        for h in range(N_HEADS):
            hs = slice(h * HEAD_DIM, (h + 1) * HEAD_DIM)
            b_col = bcol_all[:, N_HEADS + h:N_HEADS + h + 1]
            b_row = brow_all[N_HEADS + h:N_HEADS + h + 1, :]
            ig_col = gcc[:, h:h + 1]
            ig_row = grc[h:h + 1, :]
            m = m_ref[h][:, :1]
            dmat = jnp.where(causal, b_col - b_row + ig_row, -jnp.inf)
            inter = b_col + m
            mt = jnp.maximum(inter, jnp.max(dmat, axis=-1, keepdims=True))
            dw = jnp.exp(dmat - mt)
            iw = jnp.exp(inter - mt)
            qc = q[rs, hs]
            kc = k[rs, hs]
            qb = qc.astype(BF16)
            vb = v[rs, hs].astype(BF16)
            sqk = _dot_nt(qb, kc.astype(BF16)) * dw
            cmat = c_ref[h]
            nvec = n_ref[h]
            num = iw * _dot(qb, cmat.astype(BF16)) + _dot(sqk.astype(BF16), vb)
            den = (iw * jnp.sum(qc * nvec, axis=-1, keepdims=True)
                   + jnp.sum(sqk, axis=-1, keepdims=True))
            hh = num / jnp.maximum(jnp.abs(den), jnp.exp(-mt))
            bl = b_col[CHUNK - 1:CHUNK, :]
            g_end = bl - b_col + ig_col
            m_new = jnp.maximum(bl + m, jnp.max(g_end, axis=0, keepdims=True))
            wk = jnp.exp(g_end - m_new)
            cs = jnp.exp(bl + m - m_new)
            kw = kc * wk
            c_ref[h] = cs * cmat + _dot_tn(kw.astype(BF16), vb)
            n_ref[h] = cs * nvec + jnp.sum(kw, axis=0, keepdims=True)
            m_ref[h] = jnp.broadcast_to(m_new, (1, 128))
            hn = hh * lax.rsqrt(jnp.mean(hh * hh, axis=-1, keepdims=True) + M_NORM_EPS) * ng[:, hs]
            o_ref[rs, hs] = (hn * o_gate[rs, hs]).astype(o_ref.dtype)


def _mlstm(proj, gates_t, conv_w, conv_b, b_i, b_f, norm_g, batch, seq, tb=128):
    nblk = seq // tb
    bias = jnp.concatenate([b_i, b_f]).astype(F32)
    bias_row = jnp.pad(bias, (0, 128 - 2 * N_HEADS))[None, :]
    bias_col = bias[:, None]
    kern = functools.partial(_mlstm_kernel, tb=tb)
    const = lambda b, i: (0, 0)
    return pl.pallas_call(
        kern,
        grid=(batch, nblk),
        in_specs=[
            pl.BlockSpec((tb, 4 * M_WIDTH), lambda b, i: (b * nblk + i, 1)),
            pl.BlockSpec((tb, 128), lambda b, i: (b * nblk + i, GATE_COL_BLOCK)),
            pl.BlockSpec((None, 8, tb), lambda b, i: (b, 0, i)),
            pl.BlockSpec((4, 2 * M_WIDTH), const),
            pl.BlockSpec((1, 2 * M_WIDTH), const),
            pl.BlockSpec((1, 128), const),
            pl.BlockSpec((8, 1), const),
            pl.BlockSpec((1, M_WIDTH), const),
        ],
        out_specs=pl.BlockSpec((tb, M_WIDTH), lambda b, i: (b * nblk + i, 0)),
        out_shape=jax.ShapeDtypeStruct((batch * seq, M_WIDTH), BF16),
        scratch_shapes=[
            pltpu.VMEM((8, 2 * M_WIDTH), F32),
            pltpu.VMEM((N_HEADS, HEAD_DIM, HEAD_DIM), F32),
            pltpu.VMEM((N_HEADS, 1, HEAD_DIM), F32),
            pltpu.VMEM((N_HEADS, 1, 128), F32),
        ],
        compiler_params=pltpu.CompilerParams(
            dimension_semantics=("parallel", "arbitrary"), vmem_limit_bytes=VMEM_LIMIT),
        name="mlstm",
    )(proj, proj, gates_t, conv_w, conv_b[None, :], bias_row, bias_col, norm_g[None, :])


def _rwkv_kernel(p_ref, mu_ref, wl_ref, w0_ref, a0_ref, kk_ref, ka_ref, rk_ref, lng_ref, lnb_ref,
                 o_ref, carry_ref, s_ref, y_ref, *, tb):
    @pl.when(pl.program_id(1) == 0)
    def _():
        carry_ref[...] = jnp.zeros_like(carry_ref)
        s_ref[...] = jnp.zeros_like(s_ref)

    p = p_ref[...]
    ext = jnp.concatenate([carry_ref[...], p], axis=0)
    carry_ref[...] = p[tb - 8:, :]
    prev = pltpu.roll(ext, 1, axis=0)[8:, :]
    pm = p + (prev - p) * mu_ref[...]
    r = pm[:, :R_WIDTH]
    k = pm[:, R_WIDTH:2 * R_WIDTH]
    v = pm[:, 2 * R_WIDTH:3 * R_WIDTH]
    lo = pm[:, 3 * R_WIDTH:]
    lane = _iota2(lo.shape, 1)
    act = jnp.where(lane < 32, jnp.tanh(lo), jnp.where(lane < 64, lo, _sigmoid(lo)))
    lora = _dot(act, wl_ref[...], HI)
    w_log = _log_sigmoid(w0_ref[...] + lora[:, :R_WIDTH]) - 0.5
    lw = -jnp.exp(w_log)
    a = _sigmoid(a0_ref[...] + lora[:, R_WIDTH:2 * R_WIDTH])
    gate = lora[:, 2 * R_WIDTH:]

    head_ones = (_iota2((R_WIDTH, R_WIDTH), 0) // HEAD_DIM
                 == _iota2((R_WIDTH, R_WIDTH), 1) // HEAD_DIM).astype(F32)
    kk = k * kk_ref[...]
    kk = kk / jnp.maximum(jnp.sqrt(_dot(kk * kk, head_ones, HI)), 1e-12)
    k2 = k * (1.0 + (a - 1.0) * ka_ref[...])
    bvec = kk * a

    row = _iota2((CHUNK, CHUNK), 0)
    col = _iota2((CHUNK, CHUNK), 1)
    incl = col <= row
    strict = col < row
    tril = incl.astype(F32)
    eye = (col == row).astype(F32)
    n_levels = int(math.log2(CHUNK))
    level_masks = [
        ((row >> (l + 1)) == (col >> (l + 1))) & (((row >> l) & 1) == 1) & (((col >> l) & 1) == 0)
        for l in range(n_levels)
    ]

    for c in range(tb // CHUNK):
        rs = slice(c * CHUNK, (c + 1) * CHUNK)
        lwc = lw[rs, :]
        g = _dot(tril, lwc, HI)
        g_last = g[CHUNK - 1:CHUNK, :]
        e_pos = jnp.exp(g)
        e_neg = jnp.exp(-g)
        e_end = jnp.exp(g_last - g)
        a_t = -kk[rs, :] * jnp.exp(g - lwc)
        r_t = r[rs, :] * e_pos
        b_t = bvec[rs, :] * e_neg
        k_t = k2[rs, :] * e_neg
        b_h = bvec[rs, :] * e_end
        k_h = k2[rs, :] * e_end
        e_last = jnp.exp(g_last)
        for h in range(N_HEADS):
            hs = slice(h * HEAD_DIM, (h + 1) * HEAD_DIM)
            ar = jnp.concatenate([a_t[:, hs], r_t[:, hs]], axis=0)
            bk = jnp.concatenate([b_t[:, hs], k_t[:, hs]], axis=0)
            mm = _dot_nt(ar, bk, HI)
            ab = jnp.where(strict, mm[:CHUNK, :CHUNK], 0.0)
            ak = jnp.where(strict, mm[:CHUNK, CHUNK:], 0.0)
            rb = jnp.where(incl, mm[CHUNK:, :CHUNK], 0.0)
            rk = jnp.where(incl, mm[CHUNK:, CHUNK:], 0.0)
            tinv = eye + jnp.where(level_masks[0], ab, 0.0)
            for l in range(1, n_levels):
                tinv = tinv + _dot(tinv, _dot(jnp.where(level_masks[l], ab, 0.0), tinv, HI), HI)
            s0 = s_ref[h]
            ah = _dot_nt(ar, s0, HI)
            vc = v[rs, hs]
            u = _dot(tinv, ah[:CHUNK, :] + _dot(ak, vc, HI), HI)
            uv = jnp.concatenate([u, vc], axis=0)
            y_ref[rs, hs] = ah[CHUNK:, :] + _dot(jnp.concatenate([rb, rk], axis=1), uv, HI)
            bkh = jnp.concatenate([b_h[:, hs], k_h[:, hs]], axis=0)
            s_ref[h] = s0 * e_last[:, hs] + _dot_tn(uv, bkh, HI)

    y = y_ref[...]
    inv_d = 1.0 / HEAD_DIM
    mean = _dot(y, head_ones, HI) * inv_d
    yc = y - mean
    var = _dot(yc * yc, head_ones, HI) * inv_d
    yn = yc * lax.rsqrt(var + R_LN_EPS) * lng_ref[...] + lnb_ref[...]
    bonus = _dot(r * k2 * rk_ref[...], head_ones, HI)
    o_ref[...] = ((yn + bonus * v) * gate).astype(o_ref.dtype)


def _rwkv(proj, mu, w0, w_up, a0, a_up, g_up, k_k, k_a, r_k, ln_g, ln_b, batch, seq, tb=128):
    nblk = seq // tb
    w_lora = jnp.zeros((128, 3 * R_WIDTH), F32)
    w_lora = w_lora.at[0:32, 0:R_WIDTH].set(w_up)
    w_lora = w_lora.at[32:64, R_WIDTH:2 * R_WIDTH].set(a_up)
    w_lora = w_lora.at[64:128, 2 * R_WIDTH:].set(g_up)
    kern = functools.partial(_rwkv_kernel, tb=tb)
    const = lambda b, i: (0, 0)
    vec = pl.BlockSpec((1, R_WIDTH), const)
    return pl.pallas_call(
        kern,
        grid=(batch, nblk),
        in_specs=[
            pl.BlockSpec((tb, R_PROJ), lambda b, i: (b * nblk + i, 0)),
            pl.BlockSpec((1, R_PROJ), const),
            pl.BlockSpec((128, 3 * R_WIDTH), const),
            vec, vec, vec, vec, vec, vec, vec,
        ],
        out_specs=pl.BlockSpec((tb, R_WIDTH), lambda b, i: (b * nblk + i, 0)),
        out_shape=jax.ShapeDtypeStruct((batch * seq, R_WIDTH), BF16),
        scratch_shapes=[
            pltpu.VMEM((8, R_PROJ), F32),
            pltpu.VMEM((N_HEADS, HEAD_DIM, HEAD_DIM), F32),
            pltpu.VMEM((tb, R_WIDTH), F32),
        ],
        compiler_params=pltpu.CompilerParams(
            dimension_semantics=("parallel", "arbitrary"), vmem_limit_bytes=VMEM_LIMIT),
        name="rwkv7",
    )(proj, mu[None, :], w_lora, w0[None, :], a0[None, :], k_k[None, :], k_a[None, :],
      r_k.reshape(1, R_WIDTH), ln_g[None, :], ln_b[None, :])


def _attn_kernel(lq1_ref, lk1_ref, lq2_ref, lk2_ref, g_ref, q_ref, k_ref, v_ref, o_ref, *, tq, lam_init):
    qi = pl.program_id(2)
    lam = (jnp.exp(jnp.sum(lq1_ref[...] * lk1_ref[...], axis=-1, keepdims=True))
           - jnp.exp(jnp.sum(lq2_ref[...] * lk2_ref[...], axis=-1, keepdims=True)) + lam_init)
    q = q_ref[...]
    lane = _iota2(q.shape, 1)
    zero = jnp.zeros_like(q)
    qa = jnp.where(lane < HEAD_DIM, q, zero)
    qb = jnp.where(lane >= HEAD_DIM, q, zero)
    keep = _iota2((tq, tq), 1) <= _iota2((tq, tq), 0)

    def online(s, vb, m, l, acc):
        m_new = jnp.maximum(m, jnp.max(s, axis=-1, keepdims=True))
        alpha = jnp.exp(m - m_new)
        pr = jnp.exp(s - m_new)
        l = alpha * l + jnp.sum(pr, axis=-1, keepdims=True)
        acc = alpha * acc + _dot(pr.astype(BF16), vb)
        return m_new, l, acc

    def step(j, carry, masked):
        m1, l1, acc1, m2, l2, acc2 = carry
        start = pl.multiple_of(j * tq, tq)
        kb = k_ref[pl.ds(start, tq), :]
        vb = v_ref[pl.ds(start, tq), :]
        s1 = _dot_nt(qa, kb)
        s2 = _dot_nt(qb, kb)
        if masked:
            s1 = jnp.where(keep, s1, -jnp.inf)
            s2 = jnp.where(keep, s2, -jnp.inf)
        m1, l1, acc1 = online(s1, vb, m1, l1, acc1)
        m2, l2, acc2 = online(s2, vb, m2, l2, acc2)
        return m1, l1, acc1, m2, l2, acc2

    neg = jnp.full((tq, 1), -1e30, F32)
    zl = jnp.zeros((tq, 1), F32)
    za = jnp.zeros((tq, 2 * HEAD_DIM), F32)
    carry = (neg, zl, za, neg, zl, za)
    carry = lax.fori_loop(0, qi, lambda j, cr: step(j, cr, False), carry)
    m1, l1, acc1, m2, l2, acc2 = step(qi, carry, True)
    o = acc1 / l1 - lam * (acc2 / l2)
    o = o * lax.rsqrt(jnp.mean(o * o, axis=-1, keepdims=True) + A_NORM_EPS) * g_ref[...]
    o_ref[...] = (o * (1.0 - lam_init)).astype(o_ref.dtype)


def _diff_attn(qkv, lq1, lk1, lq2, lk2, norm_g, lam_init, batch, seq, tq=256):
    nq = seq // tq
    kern = functools.partial(_attn_kernel, tq=tq, lam_init=lam_init)
    const = lambda b, h, i: (0, 0)
    lvec = pl.BlockSpec((1, HEAD_DIM), const)
    return pl.pallas_call(
        kern,
        grid=(batch, N_HEADS, nq),
        in_specs=[
            lvec, lvec, lvec, lvec,
            pl.BlockSpec((1, 128), lambda b, h, i: (0, h)),
            pl.BlockSpec((tq, 128), lambda b, h, i: (b * nq + i, h)),
            pl.BlockSpec((seq, 128), lambda b, h, i: (b, N_HEADS + h)),
            pl.BlockSpec((seq, 128), lambda b, h, i: (b, 2 * N_HEADS + h)),
        ],
        out_specs=pl.BlockSpec((tq, 128), lambda b, h, i: (b * nq + i, h)),
        out_shape=jax.ShapeDtypeStruct((batch * seq, A_WIDTH), BF16),
        compiler_params=pltpu.CompilerParams(
            dimension_semantics=("parallel", "parallel", "arbitrary"), vmem_limit_bytes=VMEM_LIMIT),
        name="diff_attn",
    )(lq1[None, :], lk1[None, :], lq2[None, :], lk2[None, :], norm_g[None, :], qkv, qkv, qkv)


def _out_ffn_kernel(x_ref, ym_ref, yr_ref, ya_ref, wo_ref, g2_ref, wu_ref, wd_ref, gf_ref, o_ref,
                    h_ref, acc_ref, *, final_norm):
    j = pl.program_id(1)

    @pl.when(j == 0)
    def _():
        xn = (x_ref[...]
              + _dot(ym_ref[...], wo_ref[0:M_WIDTH, :])
              + _dot(yr_ref[...], wo_ref[M_WIDTH:M_WIDTH + R_WIDTH, :])
              + _dot(ya_ref[...], wo_ref[M_WIDTH + R_WIDTH:, :]))
        acc_ref[...] = xn
        hn = xn * lax.rsqrt(jnp.mean(xn * xn, axis=-1, keepdims=True) + NORM_EPS) * g2_ref[...]
        h_ref[...] = hn.astype(BF16)

    u = jnp.maximum(_dot(h_ref[...], wu_ref[...]), 0.0)
    acc_ref[...] += _dot((u * u).astype(BF16), wd_ref[...])

    @pl.when(j == pl.num_programs(1) - 1)
    def _():
        y = acc_ref[...]
        if final_norm:
            y = y * lax.rsqrt(jnp.mean(y * y, axis=-1, keepdims=True) + NORM_EPS) * gf_ref[...]
        o_ref[...] = y


def _out_ffn(x, ym, yr, ya, w_out, g2, w_up, w_down, g_final, final_norm, tm=1024, tf=512):
    t, d = x.shape
    f = w_up.shape[1]
    kern = functools.partial(_out_ffn_kernel, final_norm=final_norm)
    rows = lambda i, j: (i, 0)
    const = lambda i, j: (0, 0)
    return pl.pallas_call(
        kern,
        grid=(t // tm, f // tf),
        in_specs=[
            pl.BlockSpec((tm, d), rows),
            pl.BlockSpec((tm, M_WIDTH), rows),
            pl.BlockSpec((tm, R_WIDTH), rows),
            pl.BlockSpec((tm, A_WIDTH), rows),
            pl.BlockSpec((d, d), const),
            pl.BlockSpec((1, d), const),
            pl.BlockSpec((d, tf), lambda i, j: (0, j)),
            pl.BlockSpec((tf, d), lambda i, j: (j, 0)),
            pl.BlockSpec((1, d), const),
        ],
        out_specs=pl.BlockSpec((tm, d), rows),
        out_shape=jax.ShapeDtypeStruct((t, d), F32),
        scratch_shapes=[pltpu.VMEM((tm, d), BF16), pltpu.VMEM((tm, d), F32)],
        compiler_params=pltpu.CompilerParams(
            dimension_semantics=("parallel", "arbitrary"), vmem_limit_bytes=VMEM_LIMIT),
        name="out_ffn",
    )(x, ym, yr, ya, w_out, g2, w_up, w_down, g_final)


def _split_in_proj(w):
    m_main = w[:, 0:4 * M_WIDTH]
    m_gate = w[:, 4 * M_WIDTH:4 * M_WIDTH + 2 * N_HEADS]
    r0 = 4 * M_WIDTH + 2 * N_HEADS
    r_all = w[:, r0:r0 + R_PROJ]
    a0 = r0 + R_PROJ
    a_q = w[:, a0:a0 + A_WIDTH] * (HEAD_DIM ** -0.5)
    a_kv = w[:, a0 + A_WIDTH:a0 + A_PROJ]
    pad = jnp.zeros((w.shape[0], 128 - 2 * N_HEADS), w.dtype)
    w_main = jnp.concatenate([r_all, m_gate, pad, m_main], axis=1).astype(BF16)
    w_attn = jnp.concatenate([a_q, a_kv], axis=1).astype(BF16)
    return w_main, w_attn


def kernel(x, norm1_g, w_in, m_conv_w, m_conv_b, m_b_i, m_b_f, m_norm_g, r_mu, r_w0, r_w_up, r_a0, r_a_up, r_g_up, r_k_k, r_k_a, r_r_k, r_ln_g, r_ln_b, a_lq1, a_lk1, a_lq2, a_lk2, a_norm_g, w_out, norm2_g, w_ff_up, w_ff_down, final_g):
    batch, seq, d = x.shape
    depth = w_in.shape[0]
    xt = x.reshape(batch * seq, d)
    for l in range(depth):
        w_main, w_attn = _split_in_proj(w_in[l])
        g1 = norm1_g[l][None, :]
        proj = _norm_matmul(xt, g1, w_main, F32)
        qkv = _norm_matmul(xt, g1, w_attn, BF16)
        gates_t = proj[:, R_PROJ:R_PROJ + 2 * N_HEADS].reshape(batch, seq, 2 * N_HEADS).transpose(0, 2, 1)
        y_m = _mlstm(proj, gates_t, m_conv_w[l], m_conv_b[l], m_b_i[l], m_b_f[l], m_norm_g[l], batch, seq)
        y_r = _rwkv(proj, r_mu[l], r_w0[l], r_w_up[l], r_a0[l], r_a_up[l], r_g_up[l],
                    r_k_k[l], r_k_a[l], r_r_k[l], r_ln_g[l], r_ln_b[l], batch, seq)
        lam_init = 0.8 - 0.6 * math.exp(-0.3 * l)
        y_a = _diff_attn(qkv, a_lq1[l], a_lk1[l], a_lq2[l], a_lk2[l], a_norm_g[l], lam_init, batch, seq)
        xt = _out_ffn(xt, y_m, y_r, y_a, w_out[l].astype(BF16), norm2_g[l][None, :],
                      w_ff_up[l].astype(BF16), w_ff_down[l].astype(BF16), final_g[None, :],
                      final_norm=(l == depth - 1))
    return xt.reshape(batch, seq, d)
```

```python
import functools
import math

import jax
import jax.numpy as jnp
from jax import lax
from jax.experimental import pallas as pl
from jax.experimental.pallas import tpu as pltpu

F32 = jnp.float32
BF16 = jnp.bfloat16
HI = lax.Precision.HIGHEST

D_MODEL = 1024
HEAD_DIM = 64
N_HEADS = 4
M_WIDTH = 256
R_WIDTH = 256
R_PROJ = 896
A_WIDTH = 512
A_PROJ = 1536
D_FF = 4096
CHUNK = 64
NORM_EPS = 1e-6
M_NORM_EPS = 1e-6
R_LN_EPS = 64e-5
A_NORM_EPS = 1e-5
MAIN_PROJ = 2048
GATE_COL_BLOCK = 7
VMEM_LIMIT = 52 * 1024 * 1024


def _dot(a, b, prec=None):
    return jnp.dot(a, b, preferred_element_type=F32, precision=prec)


def _dot_nt(a, b, prec=None):
    return lax.dot_general(a, b, (((1,), (1,)), ((), ())), preferred_element_type=F32, precision=prec)


def _dot_tn(a, b, prec=None):
    return lax.dot_general(a, b, (((0,), (0,)), ((), ())), preferred_element_type=F32, precision=prec)


def _split3(x):
    hi = x.astype(BF16)
    rem = x - hi.astype(F32)
    mid = rem.astype(BF16)
    lo = (rem - mid.astype(F32)).astype(BF16)
    return hi, mid, lo


def _dot_f32_lhs(a, b_bf16):
    hi, mid, lo = _split3(a)
    return _dot(hi, b_bf16) + _dot(mid, b_bf16) + _dot(lo, b_bf16)


def _dot_f32_rhs(a_bf16, b):
    hi, mid, lo = _split3(b)
    return _dot(a_bf16, hi) + _dot(a_bf16, mid) + _dot(a_bf16, lo)


def _dot_x3(a, b):
    ah = a.astype(BF16)
    bh = b.astype(BF16)
    al = (a - ah.astype(F32)).astype(BF16)
    bl = (b - bh.astype(F32)).astype(BF16)
    return _dot(ah, bh) + _dot(ah, bl) + _dot(al, bh)


def _sigmoid(x):
    return 1.0 / (1.0 + jnp.exp(-x))


def _log_sigmoid(x):
    return jnp.minimum(x, 0.0) - jnp.log(1.0 + jnp.exp(-jnp.abs(x)))


def _iota2(shape, dim):
    return lax.broadcasted_iota(jnp.int32, shape, dim)


def _norm_matmul_kernel(x_ref, g_ref, w_ref, o_ref, h_ref):
    @pl.when(pl.program_id(1) == 0)
    def _():
        x = x_ref[...]
        y = x * lax.rsqrt(jnp.mean(x * x, axis=-1, keepdims=True) + NORM_EPS) * g_ref[...]
        h_ref[...] = y.astype(BF16)

    o_ref[...] = _dot(h_ref[...], w_ref[...]).astype(o_ref.dtype)


def _norm_matmul(x, g, w, out_dtype, tm=1024, tn=512):
    t, d = x.shape
    n = w.shape[1]
    return pl.pallas_call(
        _norm_matmul_kernel,
        grid=(t // tm, n // tn),
        in_specs=[
            pl.BlockSpec((tm, d), lambda i, j: (i, 0)),
            pl.BlockSpec((1, d), lambda i, j: (0, 0)),
            pl.BlockSpec((d, tn), lambda i, j: (0, j)),
        ],
        out_specs=pl.BlockSpec((tm, tn), lambda i, j: (i, j)),
        out_shape=jax.ShapeDtypeStruct((t, n), out_dtype),
        scratch_shapes=[pltpu.VMEM((tm, d), BF16)],
        compiler_params=pltpu.CompilerParams(
            dimension_semantics=("parallel", "arbitrary"), vmem_limit_bytes=VMEM_LIMIT),
        name="norm_matmul",
    )(x, g, w)


def _mlstm_kernel(x_ref, gc_ref, gr_ref, cw_ref, cb_ref, bc_ref, br_ref, ng_ref, o_ref,
                  carry_ref, c_ref, n_ref, m_ref, *, tb):
    @pl.when(pl.program_id(1) == 0)
    def _():
        carry_ref[...] = jnp.zeros_like(carry_ref)
        c_ref[...] = jnp.zeros_like(c_ref)
        n_ref[...] = jnp.zeros_like(n_ref)
        m_ref[...] = jnp.zeros_like(m_ref)

    x = x_ref[...]
    qk_pre = x[:, :2 * M_WIDTH]
    ext = jnp.concatenate([carry_ref[...], qk_pre], axis=0)
    carry_ref[...] = qk_pre[tb - 8:, :]
    cw = cw_ref[...]
    acc = cb_ref[...] + cw[3:4, :] * qk_pre
    for j in (1, 2, 3):
        acc = acc + cw[3 - j:4 - j, :] * pltpu.roll(ext, j, axis=0)[8:, :]
    qk = acc * _sigmoid(acc)
    q = qk[:, :M_WIDTH]
    k = qk[:, M_WIDTH:] * (HEAD_DIM ** -0.5)
    v = x[:, 2 * M_WIDTH:3 * M_WIDTH]
    o_gate = _sigmoid(x[:, 3 * M_WIDTH:])

    gc = gc_ref[...] + bc_ref[...]
    gc = jnp.where(_iota2(gc.shape, 1) < N_HEADS, gc, _log_sigmoid(gc))
    gr = gr_ref[...] + br_ref[...]
    gr = jnp.where(_iota2(gr.shape, 0) < N_HEADS, gr, _log_sigmoid(gr))

    row = _iota2((CHUNK, CHUNK), 0)
    col = _iota2((CHUNK, CHUNK), 1)
    causal = col <= row
    tril = causal.astype(F32)
    triu = (row <= col).astype(F32)
    ng = ng_ref[...]

    for c in range(tb // CHUNK):
        rs = slice(c * CHUNK, (c + 1) * CHUNK)
        gcc = gc[rs, :]
        grc = gr[:, rs]
        bcol_all = _dot(tril, gcc, HI)
        brow_all = _dot(grc, triu, HI)
        for h in range(N_HEADS):
            hs = slice(h * HEAD_DIM, (h + 1) * HEAD_DIM)
            b_col = bcol_all[:, N_HEADS + h:N_HEADS + h + 1]
            b_row = brow_all[N_HEADS + h:N_HEADS + h + 1, :]
            ig_col = gcc[:, h:h + 1]
            ig_row = grc[h:h + 1, :]
            m = m_ref[h][:, :1]
            dmat = jnp.where(causal, b_col - b_row + ig_row, -jnp.inf)
            inter = b_col + m
            mt = jnp.maximum(inter, jnp.max(dmat, axis=-1, keepdims=True))
            dw = jnp.exp(dmat - mt)
            iw = jnp.exp(inter - mt)
            qc = q[rs, hs]
            kc = k[rs, hs]
            qb = qc.astype(BF16)
            vb = v[rs, hs].astype(BF16)
            sqk = _dot_nt(qb, kc.astype(BF16)) * dw
            cmat = c_ref[h]
            nvec = n_ref[h]
            num = iw * _dot(qb, cmat.astype(BF16)) + _dot(sqk.astype(BF16), vb)
            den = (iw * jnp.sum(qc * nvec, axis=-1, keepdims=True)
                   + jnp.sum(sqk, axis=-1, keepdims=True))
            hh = num / jnp.maximum(jnp.abs(den), jnp.exp(-mt))
            bl = b_col[CHUNK - 1:CHUNK, :]
            g_end = bl - b_col + ig_col
            m_new = jnp.maximum(bl + m, jnp.max(g_end, axis=0, keepdims=True))
            wk = jnp.exp(g_end - m_new)
            cs = jnp.exp(bl + m - m_new)
            kw = kc * wk
            c_ref[h] = cs * cmat + _dot_tn(kw.astype(BF16), vb)
            n_ref[h] = cs * nvec + jnp.sum(kw, axis=0, keepdims=True)
            m_ref[h] = jnp.broadcast_to(m_new, (1, 128))
            hn = hh * lax.rsqrt(jnp.mean(hh * hh, axis=-1, keepdims=True) + M_NORM_EPS) * ng[:, hs]
            o_ref[rs, hs] = (hn * o_gate[rs, hs]).astype(o_ref.dtype)


def _mlstm(proj, gates_t, conv_w, conv_b, b_i, b_f, norm_g, batch, seq, tb=128):
    nblk = seq // tb
    bias = jnp.concatenate([b_i, b_f]).astype(F32)
    bias_row = jnp.pad(bias, (0, 128 - 2 * N_HEADS))[None, :]
    bias_col = bias[:, None]
    kern = functools.partial(_mlstm_kernel, tb=tb)
    const = lambda b, i: (0, 0)
    return pl.pallas_call(
        kern,
        grid=(batch, nblk),
        in_specs=[
            pl.BlockSpec((tb, 4 * M_WIDTH), lambda b, i: (b * nblk + i, 1)),
            pl.BlockSpec((tb, 128), lambda b, i: (b * nblk + i, GATE_COL_BLOCK)),
            pl.BlockSpec((None, 8, tb), lambda b, i: (b, 0, i)),
            pl.BlockSpec((4, 2 * M_WIDTH), const),
            pl.BlockSpec((1, 2 * M_WIDTH), const),
            pl.BlockSpec((1, 128), const),
            pl.BlockSpec((8, 1), const),
            pl.BlockSpec((1, M_WIDTH), const),
        ],
        out_specs=pl.BlockSpec((tb, M_WIDTH), lambda b, i: (b * nblk + i, 0)),
        out_shape=jax.ShapeDtypeStruct((batch * seq, M_WIDTH), BF16),
        scratch_shapes=[
            pltpu.VMEM((8, 2 * M_WIDTH), F32),
            pltpu.VMEM((N_HEADS, HEAD_DIM, HEAD_DIM), F32),
            pltpu.VMEM((N_HEADS, 1, HEAD_DIM), F32),
            pltpu.VMEM((N_HEADS, 1, 128), F32),
        ],
        compiler_params=pltpu.CompilerParams(
            dimension_semantics=("parallel", "arbitrary"), vmem_limit_bytes=VMEM_LIMIT),
        name="mlstm",
    )(proj, proj, gates_t, conv_w, conv_b[None, :], bias_row, bias_col, norm_g[None, :])


def _rwkv_kernel(p_ref, mu_ref, wl_ref, w0_ref, a0_ref, kk_ref, ka_ref, rk_ref, lng_ref, lnb_ref,
                 o_ref, carry_ref, s_ref, y_ref, *, tb):
    @pl.when(pl.program_id(1) == 0)
    def _():
        carry_ref[...] = jnp.zeros_like(carry_ref)
        s_ref[...] = jnp.zeros_like(s_ref)

    p = p_ref[...]
    ext = jnp.concatenate([carry_ref[...], p], axis=0)
    carry_ref[...] = p[tb - 8:, :]
    prev = pltpu.roll(ext, 1, axis=0)[8:, :]
    pm = p + (prev - p) * mu_ref[...]
    r = pm[:, :R_WIDTH]
    k = pm[:, R_WIDTH:2 * R_WIDTH]
    v = pm[:, 2 * R_WIDTH:3 * R_WIDTH]
    lo = pm[:, 3 * R_WIDTH:]
    lane = _iota2(lo.shape, 1)
    act = jnp.where(lane < 32, jnp.tanh(lo), jnp.where(lane < 64, lo, _sigmoid(lo)))
    lora = _dot_x3(act, wl_ref[...])
    w_log = _log_sigmoid(w0_ref[...] + lora[:, :R_WIDTH]) - 0.5
    lw = -jnp.exp(w_log)
    a = _sigmoid(a0_ref[...] + lora[:, R_WIDTH:2 * R_WIDTH])
    gate = lora[:, 2 * R_WIDTH:]

    n = N_HEADS * CHUNK
    row = _iota2((n, n), 0)
    col = _iota2((n, n), 1)
    same_head = (row >> 6) == (col >> 6)
    t_row = row & (CHUNK - 1)
    t_col = col & (CHUNK - 1)
    incl = same_head & (t_col <= t_row)
    strict = same_head & (t_col < t_row)
    eye = row == col
    eye_f = eye.astype(F32)
    head_ones = same_head.astype(BF16)
    n_levels = int(math.log2(CHUNK))
    level_masks = [
        same_head & ((t_row >> (l + 1)) == (t_col >> (l + 1)))
        & (((t_row >> l) & 1) == 1) & (((t_col >> l) & 1) == 0)
        for l in range(n_levels)
    ]
    tril64 = (_iota2((CHUNK, CHUNK), 1) <= _iota2((CHUNK, CHUNK), 0)).astype(BF16)

    kk = k * kk_ref[...]
    kk = kk / jnp.maximum(jnp.sqrt(_dot_f32_lhs(kk * kk, head_ones)), 1e-12)
    k2 = k * (1.0 + (a - 1.0) * ka_ref[...])
    bvec = kk * a

    def tile4(z):
        return jnp.concatenate([z, z, z, z], axis=0)

    def block_diag(z):
        return jnp.where(same_head, tile4(z), 0.0).astype(BF16)

    for c in range(tb // CHUNK):
        rs = slice(c * CHUNK, (c + 1) * CHUNK)
        lwc = lw[rs, :]
        g = _dot_f32_rhs(tril64, lwc)
        g_last = g[CHUNK - 1:CHUNK, :]
        e_pos = jnp.exp(g)
        e_neg = jnp.exp(-g)
        e_end = jnp.exp(g_last - g)
        e_last = jnp.exp(g_last)
        ar = jnp.concatenate([block_diag(-kk[rs, :] * jnp.exp(g - lwc)),
                              block_diag(r[rs, :] * e_pos)], axis=0)
        bk = jnp.concatenate([tile4((bvec[rs, :] * e_neg).astype(BF16)),
                              tile4((k2[rs, :] * e_neg).astype(BF16))], axis=0)
        bkh = jnp.concatenate([tile4((bvec[rs, :] * e_end).astype(BF16)),
                               tile4((k2[rs, :] * e_end).astype(BF16))], axis=0)
        v_bd = block_diag(v[rs, :])
        mm = _dot_nt(ar, bk)
        ab = jnp.where(strict, mm[:n, :n], 0.0)
        ak = jnp.where(strict, mm[:n, n:], 0.0).astype(BF16)
        rbk = jnp.concatenate([jnp.where(incl, mm[n:, :n], 0.0).astype(BF16),
                               jnp.where(incl, mm[n:, n:], 0.0).astype(BF16)], axis=1)
        tinv = eye_f + jnp.where(level_masks[0], ab, 0.0)
        for l in range(1, n_levels):
            t_bf = tinv.astype(BF16)
            join = jnp.where(level_masks[l], ab, 0.0).astype(BF16)
            tinv = tinv + _dot(t_bf, _dot(join, t_bf).astype(BF16))
        akv = _dot(ak, v_bd)
        h0 = s_ref[...]
        ah = _dot(ar, h0.astype(BF16))
        u = _dot(tinv.astype(BF16), (ah[:n, :] + akv).astype(BF16))
        uv = jnp.concatenate([u.astype(BF16), v_bd], axis=0)
        o = ah[n:, :] + _dot(rbk, uv)
        y_ref[rs, :] = (o[0:CHUNK, :] + o[CHUNK:2 * CHUNK, :]
                        + o[2 * CHUNK:3 * CHUNK, :] + o[3 * CHUNK:, :])
        e_last_col = jnp.sum(jnp.where(eye, e_last, 0.0), axis=-1, keepdims=True)
        s_ref[...] = h0 * e_last_col + jnp.where(same_head, _dot_tn(bkh, uv), 0.0)

    y = y_ref[...]
    inv_d = 1.0 / HEAD_DIM
    mean = _dot_f32_lhs(y, head_ones) * inv_d
    yc = y - mean
    var = _dot_f32_lhs(yc * yc, head_ones) * inv_d
    yn = yc * lax.rsqrt(var + R_LN_EPS) * lng_ref[...] + lnb_ref[...]
    bonus = _dot_f32_lhs(r * k2 * rk_ref[...], head_ones)
    o_ref[...] = ((yn + bonus * v) * gate).astype(o_ref.dtype)


def _rwkv(proj, mu, w0, w_up, a0, a_up, g_up, k_k, k_a, r_k, ln_g, ln_b, batch, seq, tb=256):
    nblk = seq // tb
    w_lora = jnp.zeros((128, 3 * R_WIDTH), F32)
    w_lora = w_lora.at[0:32, 0:R_WIDTH].set(w_up)
    w_lora = w_lora.at[32:64, R_WIDTH:2 * R_WIDTH].set(a_up)
    w_lora = w_lora.at[64:128, 2 * R_WIDTH:].set(g_up)
    kern = functools.partial(_rwkv_kernel, tb=tb)
    const = lambda b, i: (0, 0)
    vec = pl.BlockSpec((1, R_WIDTH), const)
    return pl.pallas_call(
        kern,
        grid=(batch, nblk),
        in_specs=[
            pl.BlockSpec((tb, R_PROJ), lambda b, i: (b * nblk + i, 0)),
            pl.BlockSpec((1, R_PROJ), const),
            pl.BlockSpec((128, 3 * R_WIDTH), const),
            vec, vec, vec, vec, vec, vec, vec,
        ],
        out_specs=pl.BlockSpec((tb, R_WIDTH), lambda b, i: (b * nblk + i, 0)),
        out_shape=jax.ShapeDtypeStruct((batch * seq, R_WIDTH), BF16),
        scratch_shapes=[
            pltpu.VMEM((8, R_PROJ), F32),
            pltpu.VMEM((R_WIDTH, R_WIDTH), F32),
            pltpu.VMEM((tb, R_WIDTH), F32),
        ],
        compiler_params=pltpu.CompilerParams(
            dimension_semantics=("parallel", "arbitrary"), vmem_limit_bytes=VMEM_LIMIT),
        name="rwkv7",
    )(proj, mu[None, :], w_lora, w0[None, :], a0[None, :], k_k[None, :], k_a[None, :],
      r_k.reshape(1, R_WIDTH), ln_g[None, :], ln_b[None, :])


def _attn_kernel(lq1_ref, lk1_ref, lq2_ref, lk2_ref, g_ref, q_ref, k_ref, v_ref, o_ref, *, tq, lam_init):
    qi = pl.program_id(2)
    lam = (jnp.exp(jnp.sum(lq1_ref[...] * lk1_ref[...], axis=-1, keepdims=True))
           - jnp.exp(jnp.sum(lq2_ref[...] * lk2_ref[...], axis=-1, keepdims=True)) + lam_init)
    q = q_ref[...]
    lane = _iota2(q.shape, 1)
    zero = jnp.zeros_like(q)
    qa = jnp.where(lane < HEAD_DIM, q, zero)
    qb = jnp.where(lane >= HEAD_DIM, q, zero)
    keep = _iota2((tq, tq), 1) <= _iota2((tq, tq), 0)

    def online(s, vb, m, l, acc):
        m_new = jnp.maximum(m, jnp.max(s, axis=-1, keepdims=True))
        alpha = jnp.exp(m - m_new)
        pr = jnp.exp(s - m_new)
        l = alpha * l + jnp.sum(pr, axis=-1, keepdims=True)
        acc = alpha * acc + _dot(pr.astype(BF16), vb)
        return m_new, l, acc

    def step(j, carry, masked):
        m1, l1, acc1, m2, l2, acc2 = carry
        start = pl.multiple_of(j * tq, tq)
        kb = k_ref[pl.ds(start, tq), :]
        vb = v_ref[pl.ds(start, tq), :]
        s1 = _dot_nt(qa, kb)
        s2 = _dot_nt(qb, kb)
        if masked:
            s1 = jnp.where(keep, s1, -jnp.inf)
            s2 = jnp.where(keep, s2, -jnp.inf)
        m1, l1, acc1 = online(s1, vb, m1, l1, acc1)
        m2, l2, acc2 = online(s2, vb, m2, l2, acc2)
        return m1, l1, acc1, m2, l2, acc2

    neg = jnp.full((tq, 1), -1e30, F32)
    zl = jnp.zeros((tq, 1), F32)
    za = jnp.zeros((tq, 2 * HEAD_DIM), F32)
    carry = (neg, zl, za, neg, zl, za)
    carry = lax.fori_loop(0, qi, lambda j, cr: step(j, cr, False), carry)
    m1, l1, acc1, m2, l2, acc2 = step(qi, carry, True)
    o = acc1 / l1 - lam * (acc2 / l2)
    o = o * lax.rsqrt(jnp.mean(o * o, axis=-1, keepdims=True) + A_NORM_EPS) * g_ref[...]
    o_ref[...] = (o * (1.0 - lam_init)).astype(o_ref.dtype)


def _diff_attn(qkv, lq1, lk1, lq2, lk2, norm_g, lam_init, batch, seq, tq=512):
    nq = seq // tq
    kern = functools.partial(_attn_kernel, tq=tq, lam_init=lam_init)
    const = lambda b, h, i: (0, 0)
    lvec = pl.BlockSpec((1, HEAD_DIM), const)
    return pl.pallas_call(
        kern,
        grid=(batch, N_HEADS, nq),
        in_specs=[
            lvec, lvec, lvec, lvec,
            pl.BlockSpec((1, 128), lambda b, h, i: (0, h)),
            pl.BlockSpec((tq, 128), lambda b, h, i: (b * nq + i, h)),
            pl.BlockSpec((seq, 128), lambda b, h, i: (b, N_HEADS + h)),
            pl.BlockSpec((seq, 128), lambda b, h, i: (b, 2 * N_HEADS + h)),
        ],
        out_specs=pl.BlockSpec((tq, 128), lambda b, h, i: (b * nq + i, h)),
        out_shape=jax.ShapeDtypeStruct((batch * seq, A_WIDTH), BF16),
        compiler_params=pltpu.CompilerParams(
            dimension_semantics=("parallel", "parallel", "arbitrary"), vmem_limit_bytes=VMEM_LIMIT),
        name="diff_attn",
    )(lq1[None, :], lk1[None, :], lq2[None, :], lk2[None, :], norm_g[None, :], qkv, qkv, qkv)


def _out_ffn_kernel(x_ref, ym_ref, yr_ref, ya_ref, wo_ref, g2_ref, wu_ref, wd_ref, gf_ref, o_ref,
                    h_ref, acc_ref, *, final_norm):
    j = pl.program_id(1)

    @pl.when(j == 0)
    def _():
        xn = (x_ref[...]
              + _dot(ym_ref[...], wo_ref[0:M_WIDTH, :])
              + _dot(yr_ref[...], wo_ref[M_WIDTH:M_WIDTH + R_WIDTH, :])
              + _dot(ya_ref[...], wo_ref[M_WIDTH + R_WIDTH:, :]))
        acc_ref[...] = xn
        hn = xn * lax.rsqrt(jnp.mean(xn * xn, axis=-1, keepdims=True) + NORM_EPS) * g2_ref[...]
        h_ref[...] = hn.astype(BF16)

    u = jnp.maximum(_dot(h_ref[...], wu_ref[...]), 0.0)
    acc_ref[...] += _dot((u * u).astype(BF16), wd_ref[...])

    @pl.when(j == pl.num_programs(1) - 1)
    def _():
        y = acc_ref[...]
        if final_norm:
            y = y * lax.rsqrt(jnp.mean(y * y, axis=-1, keepdims=True) + NORM_EPS) * gf_ref[...]
        o_ref[...] = y


def _out_ffn(x, ym, yr, ya, w_out, g2, w_up, w_down, g_final, final_norm, tm=1024, tf=512):
    t, d = x.shape
    f = w_up.shape[1]
    kern = functools.partial(_out_ffn_kernel, final_norm=final_norm)
    rows = lambda i, j: (i, 0)
    const = lambda i, j: (0, 0)
    return pl.pallas_call(
        kern,
        grid=(t // tm, f // tf),
        in_specs=[
            pl.BlockSpec((tm, d), rows),
            pl.BlockSpec((tm, M_WIDTH), rows),
            pl.BlockSpec((tm, R_WIDTH), rows),
            pl.BlockSpec((tm, A_WIDTH), rows),
            pl.BlockSpec((d, d), const),
            pl.BlockSpec((1, d), const),
            pl.BlockSpec((d, tf), lambda i, j: (0, j)),
            pl.BlockSpec((tf, d), lambda i, j: (j, 0)),
            pl.BlockSpec((1, d), const),
        ],
        out_specs=pl.BlockSpec((tm, d), rows),
        out_shape=jax.ShapeDtypeStruct((t, d), F32),
        scratch_shapes=[pltpu.VMEM((tm, d), BF16), pltpu.VMEM((tm, d), F32)],
        compiler_params=pltpu.CompilerParams(
            dimension_semantics=("parallel", "arbitrary"), vmem_limit_bytes=VMEM_LIMIT),
        name="out_ffn",
    )(x, ym, yr, ya, w_out, g2, w_up, w_down, g_final)


def _split_in_proj(w):
    m_main = w[:, 0:4 * M_WIDTH]
    m_gate = w[:, 4 * M_WIDTH:4 * M_WIDTH + 2 * N_HEADS]
    r0 = 4 * M_WIDTH + 2 * N_HEADS
    r_all = w[:, r0:r0 + R_PROJ]
    a0 = r0 + R_PROJ
    a_q = w[:, a0:a0 + A_WIDTH] * (HEAD_DIM ** -0.5)
    a_kv = w[:, a0 + A_WIDTH:a0 + A_PROJ]
    pad = jnp.zeros((w.shape[0], 128 - 2 * N_HEADS), w.dtype)
    w_main = jnp.concatenate([r_all, m_gate, pad, m_main], axis=1).astype(BF16)
    w_attn = jnp.concatenate([a_q, a_kv], axis=1).astype(BF16)
    return w_main, w_attn


def kernel(x, norm1_g, w_in, m_conv_w, m_conv_b, m_b_i, m_b_f, m_norm_g, r_mu, r_w0, r_w_up, r_a0, r_a_up, r_g_up, r_k_k, r_k_a, r_r_k, r_ln_g, r_ln_b, a_lq1, a_lk1, a_lq2, a_lk2, a_norm_g, w_out, norm2_g, w_ff_up, w_ff_down, final_g):
    batch, seq, d = x.shape
    depth = w_in.shape[0]
    xt = x.reshape(batch * seq, d)
    for l in range(depth):
        w_main, w_attn = _split_in_proj(w_in[l])
        g1 = norm1_g[l][None, :]
        proj = _norm_matmul(xt, g1, w_main, F32)
        qkv = _norm_matmul(xt, g1, w_attn, BF16)
        gates_t = proj[:, R_PROJ:R_PROJ + 2 * N_HEADS].reshape(batch, seq, 2 * N_HEADS).transpose(0, 2, 1)
        y_m = _mlstm(proj, gates_t, m_conv_w[l], m_conv_b[l], m_b_i[l], m_b_f[l], m_norm_g[l], batch, seq)
        y_r = _rwkv(proj, r_mu[l], r_w0[l], r_w_up[l], r_a0[l], r_a_up[l], r_g_up[l],
                    r_k_k[l], r_k_a[l], r_r_k[l], r_ln_g[l], r_ln_b[l], batch, seq)
        lam_init = 0.8 - 0.6 * math.exp(-0.3 * l)
        y_a = _diff_attn(qkv, a_lq1[l], a_lk1[l], a_lq2[l], a_lk2[l], a_norm_g[l], lam_init, batch, seq)
        xt = _out_ffn(xt, y_m, y_r, y_a, w_out[l].astype(BF16), norm2_g[l][None, :],
                      w_ff_up[l].astype(BF16), w_ff_down[l].astype(BF16), final_g[None, :],
                      final_norm=(l == depth - 1))
    return xt.reshape(batch, seq, d)
```

```python
import functools
import math

import jax
import jax.numpy as jnp
from jax import lax
from jax.experimental import pallas as pl
from jax.experimental.pallas import tpu as pltpu

F32 = jnp.float32
BF16 = jnp.bfloat16
HI = lax.Precision.HIGHEST

D_MODEL = 1024
HEAD_DIM = 64
N_HEADS = 4
M_WIDTH = 256
R_WIDTH = 256
R_PROJ = 896
A_WIDTH = 512
A_PROJ = 1536
D_FF = 4096
CHUNK = 64
NORM_EPS = 1e-6
M_NORM_EPS = 1e-6
R_LN_EPS = 64e-5
A_NORM_EPS = 1e-5
MAIN_PROJ = 2048
GATE_COL_BLOCK = 7
VMEM_LIMIT = 52 * 1024 * 1024


def _dot(a, b, prec=None):
    return jnp.dot(a, b, preferred_element_type=F32, precision=prec)


def _dot_nt(a, b, prec=None):
    return lax.dot_general(a, b, (((1,), (1,)), ((), ())), preferred_element_type=F32, precision=prec)


def _dot_tn(a, b, prec=None):
    return lax.dot_general(a, b, (((0,), (0,)), ((), ())), preferred_element_type=F32, precision=prec)


def _split3(x):
    hi = x.astype(BF16)
    rem = x - hi.astype(F32)
    mid = rem.astype(BF16)
    lo = (rem - mid.astype(F32)).astype(BF16)
    return hi, mid, lo


def _dot_f32_lhs(a, b_bf16):
    hi, mid, lo = _split3(a)
    return _dot(hi, b_bf16) + _dot(mid, b_bf16) + _dot(lo, b_bf16)


def _dot_f32_rhs(a_bf16, b):
    hi, mid, lo = _split3(b)
    return _dot(a_bf16, hi) + _dot(a_bf16, mid) + _dot(a_bf16, lo)


def _dot_x3(a, b):
    ah = a.astype(BF16)
    bh = b.astype(BF16)
    al = (a - ah.astype(F32)).astype(BF16)
    bl = (b - bh.astype(F32)).astype(BF16)
    return _dot(ah, bh) + _dot(ah, bl) + _dot(al, bh)


def _sigmoid(x):
    return 1.0 / (1.0 + jnp.exp(-x))


def _log_sigmoid(x):
    return jnp.minimum(x, 0.0) - jnp.log(1.0 + jnp.exp(-jnp.abs(x)))


def _iota2(shape, dim):
    return lax.broadcasted_iota(jnp.int32, shape, dim)


def _norm_matmul_kernel(x_ref, g_ref, w_ref, o_ref, h_ref):
    @pl.when(pl.program_id(1) == 0)
    def _():
        x = x_ref[...]
        y = x * lax.rsqrt(jnp.mean(x * x, axis=-1, keepdims=True) + NORM_EPS) * g_ref[...]
        h_ref[...] = y.astype(BF16)

    o_ref[...] = _dot(h_ref[...], w_ref[...]).astype(o_ref.dtype)


def _norm_matmul(x, g, w, out_dtype, tm=1024, tn=512):
    t, d = x.shape
    n = w.shape[1]
    return pl.pallas_call(
        _norm_matmul_kernel,
        grid=(t // tm, n // tn),
        in_specs=[
            pl.BlockSpec((tm, d), lambda i, j: (i, 0)),
            pl.BlockSpec((1, d), lambda i, j: (0, 0)),
            pl.BlockSpec((d, tn), lambda i, j: (0, j)),
        ],
        out_specs=pl.BlockSpec((tm, tn), lambda i, j: (i, j)),
        out_shape=jax.ShapeDtypeStruct((t, n), out_dtype),
        scratch_shapes=[pltpu.VMEM((tm, d), BF16)],
        compiler_params=pltpu.CompilerParams(
            dimension_semantics=("parallel", "arbitrary"), vmem_limit_bytes=VMEM_LIMIT),
        name="norm_matmul",
    )(x, g, w)


def _mlstm_kernel(x_ref, gc_ref, gr_ref, cw_ref, cb_ref, bc_ref, br_ref, ng_ref, o_ref,
                  carry_ref, c_ref, n_ref, m_ref, *, tb):
    @pl.when(pl.program_id(1) == 0)
    def _():
        carry_ref[...] = jnp.zeros_like(carry_ref)
        c_ref[...] = jnp.zeros_like(c_ref)
        n_ref[...] = jnp.zeros_like(n_ref)
        m_ref[...] = jnp.zeros_like(m_ref)

    x = x_ref[...]
    qk_pre = x[:, :2 * M_WIDTH]
    ext = jnp.concatenate([carry_ref[...], qk_pre], axis=0)
    carry_ref[...] = qk_pre[tb - 8:, :]
    cw = cw_ref[...]
    acc = cb_ref[...] + cw[3:4, :] * qk_pre
    for j in (1, 2, 3):
        acc = acc + cw[3 - j:4 - j, :] * pltpu.roll(ext, j, axis=0)[8:, :]
    qk = acc * _sigmoid(acc)
    q = qk[:, :M_WIDTH]
    k = qk[:, M_WIDTH:] * (HEAD_DIM ** -0.5)
    v = x[:, 2 * M_WIDTH:3 * M_WIDTH]
    o_gate = _sigmoid(x[:, 3 * M_WIDTH:])

    n = N_HEADS * CHUNK
    row = _iota2((n, n), 0)
    col = _iota2((n, n), 1)
    same_head = (row >> 6) == (col >> 6)
    causal = same_head & ((col & (CHUNK - 1)) <= (row & (CHUNK - 1)))
    eye = row == col
    triu_bd = (same_head & ((row & (CHUNK - 1)) <= (col & (CHUNK - 1)))).astype(BF16)
    tril64 = (_iota2((CHUNK, CHUNK), 1) <= _iota2((CHUNK, CHUNK), 0)).astype(BF16)
    lane_g = _iota2((n, 128), 1)
    head_g = _iota2((n, 128), 0) >> 6
    pick_i = lane_g == head_g
    pick_f = lane_g == head_g + N_HEADS
    ng = ng_ref[...]

    def tile4(z):
        return jnp.concatenate([z, z, z, z], axis=0)

    def col_of(z, pick):
        zz = tile4(z) if z.shape[0] == CHUNK else z
        return jnp.sum(jnp.where(pick, zz, 0.0), axis=-1, keepdims=True)

    def to_row(z_col):
        return jnp.sum(jnp.where(eye, z_col, 0.0), axis=0, keepdims=True)

    gc = gc_ref[...] + bc_ref[...]
    gc = jnp.where(_iota2(gc.shape, 1) < N_HEADS, gc, _log_sigmoid(gc))
    gr = gr_ref[...] + br_ref[...]
    ig_rows = gr[:, :n]
    b_rows = _dot_f32_lhs(_log_sigmoid(gr[:, n:]), triu_bd)

    m_col = m_ref[...]
    pre = []
    for c in range(tb // CHUNK):
        rs = slice(c * CHUNK, (c + 1) * CHUNK)
        gcc = gc[rs, :]
        bc = _dot_f32_rhs(tril64, gcc)
        b_col = col_of(bc, pick_f)
        ig_col = col_of(gcc, pick_i)
        bl_col = col_of(bc[CHUNK - 1:CHUNK, :], pick_f)
        g_end = bl_col - b_col + ig_col
        seg_max = jnp.concatenate(
            [jnp.broadcast_to(jnp.max(g_end[h * CHUNK:(h + 1) * CHUNK, :], axis=0, keepdims=True), (CHUNK, 1))
             for h in range(N_HEADS)], axis=0)
        m_new = jnp.maximum(bl_col + m_col, seg_max)
        pre.append(dict(rs=rs, b_col=b_col, m_col=m_col, wk=jnp.exp(g_end - m_new),
                        cs_col=jnp.exp(bl_col + m_col - m_new),
                        b_row=b_rows[c:c + 1, :], ig_row=ig_rows[c:c + 1, :]))
        m_col = m_new
    m_ref[...] = m_col

    for ch in pre:
        rs = ch["rs"]
        dmat = jnp.where(causal, ch["b_col"] - ch["b_row"] + ch["ig_row"], -jnp.inf)
        inter = ch["b_col"] + ch["m_col"]
        mt = jnp.maximum(inter, jnp.max(dmat, axis=-1, keepdims=True))
        dw = jnp.exp(dmat - mt)
        iw = jnp.exp(inter - mt)
        q_f = jnp.where(same_head, tile4(q[rs, :]), 0.0)
        k_f = jnp.where(same_head, tile4(k[rs, :]), 0.0)
        q_bd = q_f.astype(BF16)
        v_bd = jnp.where(same_head, tile4(v[rs, :]), 0.0).astype(BF16)
        sqk = _dot_nt(q_bd, tile4(k[rs, :].astype(BF16))) * dw
        cmat = c_ref[...]
        nvec = n_ref[...]
        num = iw * _dot(q_bd, cmat.astype(BF16)) + _dot(sqk.astype(BF16), v_bd)
        den = (iw * jnp.sum(q_f * nvec, axis=-1, keepdims=True)
               + jnp.sum(sqk, axis=-1, keepdims=True))
        hh = num / jnp.maximum(jnp.abs(den), jnp.exp(-mt))
        kw = k_f * ch["wk"]
        c_ref[...] = ch["cs_col"] * cmat + _dot_tn(kw.astype(BF16), v_bd)
        n_ref[...] = to_row(ch["cs_col"]) * nvec + jnp.sum(kw, axis=0, keepdims=True)
        hn = hh * lax.rsqrt(jnp.sum(hh * hh, axis=-1, keepdims=True) * (1.0 / HEAD_DIM) + M_NORM_EPS)
        hc = (hn[0:CHUNK, :] + hn[CHUNK:2 * CHUNK, :] + hn[2 * CHUNK:3 * CHUNK, :] + hn[3 * CHUNK:, :])
        o_ref[rs, :] = (hc * ng * o_gate[rs, :]).astype(o_ref.dtype)


def _mlstm(proj, gates_r, conv_w, conv_b, b_i, b_f, norm_g, batch, seq, tb=512):
    nblk = seq // tb
    bias = jnp.concatenate([b_i, b_f]).astype(F32)
    bias_lanes = jnp.pad(bias, (0, 128 - 2 * N_HEADS))[None, :]
    bias_r = jnp.repeat(bias, CHUNK)[None, :]
    kern = functools.partial(_mlstm_kernel, tb=tb)
    const = lambda b, i: (0, 0)
    return pl.pallas_call(
        kern,
        grid=(batch, nblk),
        in_specs=[
            pl.BlockSpec((tb, 4 * M_WIDTH), lambda b, i: (b * nblk + i, 1)),
            pl.BlockSpec((tb, 128), lambda b, i: (b * nblk + i, GATE_COL_BLOCK)),
            pl.BlockSpec((tb // CHUNK, 2 * N_HEADS * CHUNK), lambda b, i: (b * nblk + i, 0)),
            pl.BlockSpec((4, 2 * M_WIDTH), const),
            pl.BlockSpec((1, 2 * M_WIDTH), const),
            pl.BlockSpec((1, 128), const),
            pl.BlockSpec((1, 2 * N_HEADS * CHUNK), const),
            pl.BlockSpec((1, M_WIDTH), const),
        ],
        out_specs=pl.BlockSpec((tb, M_WIDTH), lambda b, i: (b * nblk + i, 0)),
        out_shape=jax.ShapeDtypeStruct((batch * seq, M_WIDTH), BF16),
        scratch_shapes=[
            pltpu.VMEM((8, 2 * M_WIDTH), F32),
            pltpu.VMEM((M_WIDTH, M_WIDTH), F32),
            pltpu.VMEM((1, M_WIDTH), F32),
            pltpu.VMEM((N_HEADS * CHUNK, 1), F32),
        ],
        compiler_params=pltpu.CompilerParams(
            dimension_semantics=("parallel", "arbitrary"), vmem_limit_bytes=VMEM_LIMIT),
        name="mlstm",
    )(proj, proj, gates_r, conv_w, conv_b[None, :], bias_lanes, bias_r, norm_g[None, :])


def _rwkv_kernel(p_ref, mu_ref, wl_ref, w0_ref, a0_ref, kk_ref, ka_ref, rk_ref, lng_ref, lnb_ref,
                 o_ref, carry_ref, s_ref, y_ref, *, tb):
    @pl.when(pl.program_id(1) == 0)
    def _():
        carry_ref[...] = jnp.zeros_like(carry_ref)
        s_ref[...] = jnp.zeros_like(s_ref)

    p = p_ref[...]
    ext = jnp.concatenate([carry_ref[...], p], axis=0)
    carry_ref[...] = p[tb - 8:, :]
    prev = pltpu.roll(ext, 1, axis=0)[8:, :]
    pm = p + (prev - p) * mu_ref[...]
    r = pm[:, :R_WIDTH]
    k = pm[:, R_WIDTH:2 * R_WIDTH]
    v = pm[:, 2 * R_WIDTH:3 * R_WIDTH]
    lo = pm[:, 3 * R_WIDTH:]
    lane = _iota2(lo.shape, 1)
    act = jnp.where(lane < 32, jnp.tanh(lo), jnp.where(lane < 64, lo, _sigmoid(lo)))
    lora = _dot_x3(act, wl_ref[...])
    w_log = _log_sigmoid(w0_ref[...] + lora[:, :R_WIDTH]) - 0.5
    lw = -jnp.exp(w_log)
    a = _sigmoid(a0_ref[...] + lora[:, R_WIDTH:2 * R_WIDTH])
    gate = lora[:, 2 * R_WIDTH:]

    n = N_HEADS * CHUNK
    row = _iota2((n, n), 0)
    col = _iota2((n, n), 1)
    same_head = (row >> 6) == (col >> 6)
    t_row = row & (CHUNK - 1)
    t_col = col & (CHUNK - 1)
    incl = same_head & (t_col <= t_row)
    strict = same_head & (t_col < t_row)
    eye = row == col
    eye_f = eye.astype(F32)
    head_ones = same_head.astype(BF16)
    n_levels = int(math.log2(CHUNK))
    level_masks = [
        same_head & ((t_row >> (l + 1)) == (t_col >> (l + 1)))
        & (((t_row >> l) & 1) == 1) & (((t_col >> l) & 1) == 0)
        for l in range(n_levels)
    ]
    tril64 = (_iota2((CHUNK, CHUNK), 1) <= _iota2((CHUNK, CHUNK), 0)).astype(BF16)

    kk = k * kk_ref[...]
    kk = kk / jnp.maximum(jnp.sqrt(_dot_f32_lhs(kk * kk, head_ones)), 1e-12)
    k2 = k * (1.0 + (a - 1.0) * ka_ref[...])
    bvec = kk * a

    def tile4(z):
        return jnp.concatenate([z, z, z, z], axis=0)

    def block_diag(z):
        return jnp.where(same_head, tile4(z), 0.0).astype(BF16)

    chunks = []
    for c in range(tb // CHUNK):
        rs = slice(c * CHUNK, (c + 1) * CHUNK)
        lwc = lw[rs, :]
        g = _dot_f32_rhs(tril64, lwc)
        g_last = g[CHUNK - 1:CHUNK, :]
        e_pos = jnp.exp(g)
        e_neg = jnp.exp(-g)
        e_end = jnp.exp(g_last - g)
        e_last = jnp.exp(g_last)
        ar = jnp.concatenate([block_diag(-kk[rs, :] * jnp.exp(g - lwc)),
                              block_diag(r[rs, :] * e_pos)], axis=0)
        bk = jnp.concatenate([tile4((bvec[rs, :] * e_neg).astype(BF16)),
                              tile4((k2[rs, :] * e_neg).astype(BF16))], axis=0)
        bkh = jnp.concatenate([tile4((bvec[rs, :] * e_end).astype(BF16)),
                               tile4((k2[rs, :] * e_end).astype(BF16))], axis=0)
        v_bd = block_diag(v[rs, :])
        mm = _dot_nt(ar, bk)
        ab = jnp.where(strict, mm[:n, :n], 0.0)
        ak = jnp.where(strict, mm[:n, n:], 0.0).astype(BF16)
        rbk = jnp.concatenate([jnp.where(incl, mm[n:, :n], 0.0).astype(BF16),
                               jnp.where(incl, mm[n:, n:], 0.0).astype(BF16)], axis=1)
        e_last_col = jnp.sum(jnp.where(eye, e_last, 0.0), axis=-1, keepdims=True)
        chunks.append(dict(rs=rs, ar=ar, bkh=bkh, v_bd=v_bd, ab=ab, rbk=rbk, akv=_dot(ak, v_bd),
                           e_last_col=e_last_col, tinv=eye_f + jnp.where(level_masks[0], ab, 0.0)))
    for l in range(1, n_levels):
        for ch in chunks:
            t_bf = ch["tinv"].astype(BF16)
            join = jnp.where(level_masks[l], ch["ab"], 0.0).astype(BF16)
            ch["tinv"] = ch["tinv"] + _dot(t_bf, _dot(join, t_bf).astype(BF16))
    for ch in chunks:
        h0 = s_ref[...]
        ah = _dot(ch["ar"], h0.astype(BF16))
        u = _dot(ch["tinv"].astype(BF16), (ah[:n, :] + ch["akv"]).astype(BF16))
        uv = jnp.concatenate([u.astype(BF16), ch["v_bd"]], axis=0)
        o = ah[n:, :] + _dot(ch["rbk"], uv)
        y_ref[ch["rs"], :] = (o[0:CHUNK, :] + o[CHUNK:2 * CHUNK, :]
                              + o[2 * CHUNK:3 * CHUNK, :] + o[3 * CHUNK:, :])
        s_ref[...] = h0 * ch["e_last_col"] + jnp.where(same_head, _dot_tn(ch["bkh"], uv), 0.0)

    y = y_ref[...]
    inv_d = 1.0 / HEAD_DIM
    mean = _dot_f32_lhs(y, head_ones) * inv_d
    yc = y - mean
    var = _dot_f32_lhs(yc * yc, head_ones) * inv_d
    yn = yc * lax.rsqrt(var + R_LN_EPS) * lng_ref[...] + lnb_ref[...]
    bonus = _dot_f32_lhs(r * k2 * rk_ref[...], head_ones)
    o_ref[...] = ((yn + bonus * v) * gate).astype(o_ref.dtype)


def _rwkv(proj, mu, w0, w_up, a0, a_up, g_up, k_k, k_a, r_k, ln_g, ln_b, batch, seq, tb=512):
    nblk = seq // tb
    w_lora = jnp.zeros((128, 3 * R_WIDTH), F32)
    w_lora = w_lora.at[0:32, 0:R_WIDTH].set(w_up)
    w_lora = w_lora.at[32:64, R_WIDTH:2 * R_WIDTH].set(a_up)
    w_lora = w_lora.at[64:128, 2 * R_WIDTH:].set(g_up)
    kern = functools.partial(_rwkv_kernel, tb=tb)
    const = lambda b, i: (0, 0)
    vec = pl.BlockSpec((1, R_WIDTH), const)
    return pl.pallas_call(
        kern,
        grid=(batch, nblk),
        in_specs=[
            pl.BlockSpec((tb, R_PROJ), lambda b, i: (b * nblk + i, 0)),
            pl.BlockSpec((1, R_PROJ), const),
            pl.BlockSpec((128, 3 * R_WIDTH), const),
            vec, vec, vec, vec, vec, vec, vec,
        ],
        out_specs=pl.BlockSpec((tb, R_WIDTH), lambda b, i: (b * nblk + i, 0)),
        out_shape=jax.ShapeDtypeStruct((batch * seq, R_WIDTH), BF16),
        scratch_shapes=[
            pltpu.VMEM((8, R_PROJ), F32),
            pltpu.VMEM((R_WIDTH, R_WIDTH), F32),
            pltpu.VMEM((tb, R_WIDTH), F32),
        ],
        compiler_params=pltpu.CompilerParams(
            dimension_semantics=("parallel", "arbitrary"), vmem_limit_bytes=VMEM_LIMIT),
        name="rwkv7",
    )(proj, mu[None, :], w_lora, w0[None, :], a0[None, :], k_k[None, :], k_a[None, :],
      r_k.reshape(1, R_WIDTH), ln_g[None, :], ln_b[None, :])


def _attn_kernel(lq1_ref, lk1_ref, lq2_ref, lk2_ref, g_ref, q_ref, k_ref, v_ref, o_ref, *, tq, lam_init):
    qi = pl.program_id(2)
    lam = (jnp.exp(jnp.sum(lq1_ref[...] * lk1_ref[...], axis=-1, keepdims=True))
           - jnp.exp(jnp.sum(lq2_ref[...] * lk2_ref[...], axis=-1, keepdims=True)) + lam_init)
    q = q_ref[...]
    lane = _iota2(q.shape, 1)
    zero = jnp.zeros_like(q)
    qa = jnp.where(lane < HEAD_DIM, q, zero)
    qb = jnp.where(lane >= HEAD_DIM, q, zero)
    keep = _iota2((tq, tq), 1) <= _iota2((tq, tq), 0)

    def online(s, vb, m, l, acc):
        m_new = jnp.maximum(m, jnp.max(s, axis=-1, keepdims=True))
        alpha = jnp.exp(m - m_new)
        pr = jnp.exp(s - m_new)
        l = alpha * l + jnp.sum(pr, axis=-1, keepdims=True)
        acc = alpha * acc + _dot(pr.astype(BF16), vb)
        return m_new, l, acc

    def step(j, carry, masked):
        m1, l1, acc1, m2, l2, acc2 = carry
        start = pl.multiple_of(j * tq, tq)
        kb = k_ref[pl.ds(start, tq), :]
        vb = v_ref[pl.ds(start, tq), :]
        s1 = _dot_nt(qa, kb)
        s2 = _dot_nt(qb, kb)
        if masked:
            s1 = jnp.where(keep, s1, -jnp.inf)
            s2 = jnp.where(keep, s2, -jnp.inf)
        m1, l1, acc1 = online(s1, vb, m1, l1, acc1)
        m2, l2, acc2 = online(s2, vb, m2, l2, acc2)
        return m1, l1, acc1, m2, l2, acc2

    neg = jnp.full((tq, 1), -1e30, F32)
    zl = jnp.zeros((tq, 1), F32)
    za = jnp.zeros((tq, 2 * HEAD_DIM), F32)
    carry = (neg, zl, za, neg, zl, za)
    carry = lax.fori_loop(0, qi, lambda j, cr: step(j, cr, False), carry)
    m1, l1, acc1, m2, l2, acc2 = step(qi, carry, True)
    o = acc1 / l1 - lam * (acc2 / l2)
    o = o * lax.rsqrt(jnp.mean(o * o, axis=-1, keepdims=True) + A_NORM_EPS) * g_ref[...]
    o_ref[...] = (o * (1.0 - lam_init)).astype(o_ref.dtype)


def _diff_attn(qkv, lq1, lk1, lq2, lk2, norm_g, lam_init, batch, seq, tq=512):
    nq = seq // tq
    kern = functools.partial(_attn_kernel, tq=tq, lam_init=lam_init)
    const = lambda b, h, i: (0, 0)
    lvec = pl.BlockSpec((1, HEAD_DIM), const)
    return pl.pallas_call(
        kern,
        grid=(batch, N_HEADS, nq),
        in_specs=[
            lvec, lvec, lvec, lvec,
            pl.BlockSpec((1, 128), lambda b, h, i: (0, h)),
            pl.BlockSpec((tq, 128), lambda b, h, i: (b * nq + i, h)),
            pl.BlockSpec((seq, 128), lambda b, h, i: (b, N_HEADS + h)),
            pl.BlockSpec((seq, 128), lambda b, h, i: (b, 2 * N_HEADS + h)),
        ],
        out_specs=pl.BlockSpec((tq, 128), lambda b, h, i: (b * nq + i, h)),
        out_shape=jax.ShapeDtypeStruct((batch * seq, A_WIDTH), BF16),
        compiler_params=pltpu.CompilerParams(
            dimension_semantics=("parallel", "parallel", "arbitrary"), vmem_limit_bytes=VMEM_LIMIT),
        name="diff_attn",
    )(lq1[None, :], lk1[None, :], lq2[None, :], lk2[None, :], norm_g[None, :], qkv, qkv, qkv)


def _out_ffn_kernel(x_ref, ym_ref, yr_ref, ya_ref, wo_ref, g2_ref, wu_ref, wd_ref, gf_ref, o_ref,
                    h_ref, acc_ref, *, final_norm):
    j = pl.program_id(1)

    @pl.when(j == 0)
    def _():
        xn = (x_ref[...]
              + _dot(ym_ref[...], wo_ref[0:M_WIDTH, :])
              + _dot(yr_ref[...], wo_ref[M_WIDTH:M_WIDTH + R_WIDTH, :])
              + _dot(ya_ref[...], wo_ref[M_WIDTH + R_WIDTH:, :]))
        acc_ref[...] = xn
        hn = xn * lax.rsqrt(jnp.mean(xn * xn, axis=-1, keepdims=True) + NORM_EPS) * g2_ref[...]
        h_ref[...] = hn.astype(BF16)

    u = jnp.maximum(_dot(h_ref[...], wu_ref[...]), 0.0)
    acc_ref[...] += _dot((u * u).astype(BF16), wd_ref[...])

    @pl.when(j == pl.num_programs(1) - 1)
    def _():
        y = acc_ref[...]
        if final_norm:
            y = y * lax.rsqrt(jnp.mean(y * y, axis=-1, keepdims=True) + NORM_EPS) * gf_ref[...]
        o_ref[...] = y


def _out_ffn(x, ym, yr, ya, w_out, g2, w_up, w_down, g_final, final_norm, tm=1024, tf=512):
    t, d = x.shape
    f = w_up.shape[1]
    kern = functools.partial(_out_ffn_kernel, final_norm=final_norm)
    rows = lambda i, j: (i, 0)
    const = lambda i, j: (0, 0)
    return pl.pallas_call(
        kern,
        grid=(t // tm, f // tf),
        in_specs=[
            pl.BlockSpec((tm, d), rows),
            pl.BlockSpec((tm, M_WIDTH), rows),
            pl.BlockSpec((tm, R_WIDTH), rows),
            pl.BlockSpec((tm, A_WIDTH), rows),
            pl.BlockSpec((d, d), const),
            pl.BlockSpec((1, d), const),
            pl.BlockSpec((d, tf), lambda i, j: (0, j)),
            pl.BlockSpec((tf, d), lambda i, j: (j, 0)),
            pl.BlockSpec((1, d), const),
        ],
        out_specs=pl.BlockSpec((tm, d), rows),
        out_shape=jax.ShapeDtypeStruct((t, d), F32),
        scratch_shapes=[pltpu.VMEM((tm, d), BF16), pltpu.VMEM((tm, d), F32)],
        compiler_params=pltpu.CompilerParams(
            dimension_semantics=("parallel", "arbitrary"), vmem_limit_bytes=VMEM_LIMIT),
        name="out_ffn",
    )(x, ym, yr, ya, w_out, g2, w_up, w_down, g_final)


def _split_in_proj(w):
    m_main = w[:, 0:4 * M_WIDTH]
    m_gate = w[:, 4 * M_WIDTH:4 * M_WIDTH + 2 * N_HEADS]
    r0 = 4 * M_WIDTH + 2 * N_HEADS
    r_all = w[:, r0:r0 + R_PROJ]
    a0 = r0 + R_PROJ
    a_q = w[:, a0:a0 + A_WIDTH] * (HEAD_DIM ** -0.5)
    a_kv = w[:, a0 + A_WIDTH:a0 + A_PROJ]
    pad = jnp.zeros((w.shape[0], 128 - 2 * N_HEADS), w.dtype)
    w_main = jnp.concatenate([r_all, m_gate, pad, m_main], axis=1).astype(BF16)
    w_attn = jnp.concatenate([a_q, a_kv], axis=1).astype(BF16)
    return w_main, w_attn


def kernel(x, norm1_g, w_in, m_conv_w, m_conv_b, m_b_i, m_b_f, m_norm_g, r_mu, r_w0, r_w_up, r_a0, r_a_up, r_g_up, r_k_k, r_k_a, r_r_k, r_ln_g, r_ln_b, a_lq1, a_lk1, a_lq2, a_lk2, a_norm_g, w_out, norm2_g, w_ff_up, w_ff_down, final_g):
    batch, seq, d = x.shape
    depth = w_in.shape[0]
    xt = x.reshape(batch * seq, d)
    for l in range(depth):
        w_main, w_attn = _split_in_proj(w_in[l])
        g1 = norm1_g[l][None, :]
        proj = _norm_matmul(xt, g1, w_main, F32, tn=MAIN_PROJ // 2)
        qkv = _norm_matmul(xt, g1, w_attn, BF16, tn=A_PROJ // 2)
        gates_r = (proj[:, R_PROJ:R_PROJ + 2 * N_HEADS]
                   .reshape(batch * seq // CHUNK, CHUNK, 2 * N_HEADS).transpose(0, 2, 1)
                   .reshape(batch * seq // CHUNK, 2 * N_HEADS * CHUNK))
        y_m = _mlstm(proj, gates_r, m_conv_w[l], m_conv_b[l], m_b_i[l], m_b_f[l], m_norm_g[l], batch, seq)
        y_r = _rwkv(proj, r_mu[l], r_w0[l], r_w_up[l], r_a0[l], r_a_up[l], r_g_up[l],
                    r_k_k[l], r_k_a[l], r_r_k[l], r_ln_g[l], r_ln_b[l], batch, seq)
        lam_init = 0.8 - 0.6 * math.exp(-0.3 * l)
        y_a = _diff_attn(qkv, a_lq1[l], a_lk1[l], a_lq2[l], a_lk2[l], a_norm_g[l], lam_init, batch, seq)
        xt = _out_ffn(xt, y_m, y_r, y_a, w_out[l].astype(BF16), norm2_g[l][None, :],
                      w_ff_up[l].astype(BF16), w_ff_down[l].astype(BF16), final_g[None, :],
                      final_norm=(l == depth - 1))
    return xt.reshape(batch, seq, d)
```

```python
import functools
import math

import jax
import jax.numpy as jnp
from jax import lax
from jax.experimental import pallas as pl
from jax.experimental.pallas import tpu as pltpu

F32 = jnp.float32
BF16 = jnp.bfloat16
HI = lax.Precision.HIGHEST

D_MODEL = 1024
HEAD_DIM = 64
N_HEADS = 4
M_WIDTH = 256
R_WIDTH = 256
R_PROJ = 896
A_WIDTH = 512
A_PROJ = 1536
D_FF = 4096
CHUNK = 64
ATTN_STRIP = 64
NORM_EPS = 1e-6
M_NORM_EPS = 1e-6
R_LN_EPS = 64e-5
A_NORM_EPS = 1e-5
MAIN_PROJ = 2048
GATE_COL_BLOCK = 7
VMEM_LIMIT = 52 * 1024 * 1024


def _dot(a, b, prec=None):
    return jnp.dot(a, b, preferred_element_type=F32, precision=prec)


def _dot_nt(a, b, prec=None):
    return lax.dot_general(a, b, (((1,), (1,)), ((), ())), preferred_element_type=F32, precision=prec)


def _dot_tn(a, b, prec=None):
    return lax.dot_general(a, b, (((0,), (0,)), ((), ())), preferred_element_type=F32, precision=prec)


def _split3(x):
    hi = x.astype(BF16)
    rem = x - hi.astype(F32)
    mid = rem.astype(BF16)
    lo = (rem - mid.astype(F32)).astype(BF16)
    return hi, mid, lo


def _dot_f32_lhs(a, b_bf16):
    hi, mid, lo = _split3(a)
    return _dot(hi, b_bf16) + _dot(mid, b_bf16) + _dot(lo, b_bf16)


def _dot_f32_rhs(a_bf16, b):
    hi, mid, lo = _split3(b)
    return _dot(a_bf16, hi) + _dot(a_bf16, mid) + _dot(a_bf16, lo)


def _dot_x3(a, b):
    ah = a.astype(BF16)
    bh = b.astype(BF16)
    al = (a - ah.astype(F32)).astype(BF16)
    bl = (b - bh.astype(F32)).astype(BF16)
    return _dot(ah, bh) + _dot(ah, bl) + _dot(al, bh)


def _sigmoid(x):
    return 1.0 / (1.0 + jnp.exp(-x))


def _log_sigmoid(x):
    return jnp.minimum(x, 0.0) - jnp.log(1.0 + jnp.exp(-jnp.abs(x)))


def _iota2(shape, dim):
    return lax.broadcasted_iota(jnp.int32, shape, dim)


def _in_proj_kernel(x_ref, g_ref, wm_ref, wa_ref, om_ref, oa_ref):
    x = x_ref[...]
    h = (x * lax.rsqrt(jnp.mean(x * x, axis=-1, keepdims=True) + NORM_EPS) * g_ref[...]).astype(BF16)
    om_ref[...] = _dot(h, wm_ref[...])
    oa_ref[...] = _dot(h, wa_ref[...]).astype(oa_ref.dtype)


def _in_proj(x, g, w_main, w_attn, tm=512):
    t, d = x.shape
    rows = lambda i: (i, 0)
    const = lambda i: (0, 0)
    return pl.pallas_call(
        _in_proj_kernel,
        grid=(t // tm,),
        in_specs=[
            pl.BlockSpec((tm, d), rows),
            pl.BlockSpec((1, d), const),
            pl.BlockSpec((d, MAIN_PROJ), const),
            pl.BlockSpec((d, A_PROJ), const),
        ],
        out_specs=[pl.BlockSpec((tm, MAIN_PROJ), rows), pl.BlockSpec((tm, A_PROJ), rows)],
        out_shape=[jax.ShapeDtypeStruct((t, MAIN_PROJ), F32), jax.ShapeDtypeStruct((t, A_PROJ), BF16)],
        compiler_params=pltpu.CompilerParams(
            dimension_semantics=("parallel",), vmem_limit_bytes=VMEM_LIMIT),
        name="in_proj",
    )(x, g, w_main, w_attn)


def _mlstm_kernel(x_ref, gc_ref, gr_ref, cw_ref, cb_ref, bc_ref, br_ref, ng_ref, o_ref,
                  carry_ref, c_ref, n_ref, m_ref, *, tb):
    @pl.when(pl.program_id(1) == 0)
    def _():
        carry_ref[...] = jnp.zeros_like(carry_ref)
        c_ref[...] = jnp.zeros_like(c_ref)
        n_ref[...] = jnp.zeros_like(n_ref)
        m_ref[...] = jnp.zeros_like(m_ref)

    x = x_ref[...]
    qk_pre = x[:, :2 * M_WIDTH]
    ext = jnp.concatenate([carry_ref[...], qk_pre], axis=0)
    carry_ref[...] = qk_pre[tb - 8:, :]
    cw = cw_ref[...]
    acc = cb_ref[...] + cw[3:4, :] * qk_pre
    for j in (1, 2, 3):
        acc = acc + cw[3 - j:4 - j, :] * pltpu.roll(ext, j, axis=0)[8:, :]
    qk = acc * _sigmoid(acc)
    q = qk[:, :M_WIDTH]
    k = qk[:, M_WIDTH:] * (HEAD_DIM ** -0.5)
    v = x[:, 2 * M_WIDTH:3 * M_WIDTH]
    o_gate = _sigmoid(x[:, 3 * M_WIDTH:])

    n = N_HEADS * CHUNK
    row = _iota2((n, n), 0)
    col = _iota2((n, n), 1)
    same_head = (row >> 6) == (col >> 6)
    causal = same_head & ((col & (CHUNK - 1)) <= (row & (CHUNK - 1)))
    eye = row == col
    triu_bd = (same_head & ((row & (CHUNK - 1)) <= (col & (CHUNK - 1)))).astype(BF16)
    tril64 = (_iota2((CHUNK, CHUNK), 1) <= _iota2((CHUNK, CHUNK), 0)).astype(BF16)
    lane_g = _iota2((n, 128), 1)
    head_g = _iota2((n, 128), 0) >> 6
    pick_i = lane_g == head_g
    pick_f = lane_g == head_g + N_HEADS
    ng = ng_ref[...]

    def tile4(z):
        return jnp.concatenate([z, z, z, z], axis=0)

    def col_of(z, pick):
        zz = tile4(z) if z.shape[0] == CHUNK else z
        return jnp.sum(jnp.where(pick, zz, 0.0), axis=-1, keepdims=True)

    def to_row(z_col):
        return jnp.sum(jnp.where(eye, z_col, 0.0), axis=0, keepdims=True)

    gc = gc_ref[...] + bc_ref[...]
    gc = jnp.where(_iota2(gc.shape, 1) < N_HEADS, gc, _log_sigmoid(gc))
    gr = gr_ref[...] + br_ref[...]
    ig_rows = gr[:, :n]
    b_rows = _dot_f32_lhs(_log_sigmoid(gr[:, n:]), triu_bd)

    m_col = m_ref[...]
    pre = []
    for c in range(tb // CHUNK):
        rs = slice(c * CHUNK, (c + 1) * CHUNK)
        gcc = gc[rs, :]
        bc = _dot_f32_rhs(tril64, gcc)
        b_col = col_of(bc, pick_f)
        ig_col = col_of(gcc, pick_i)
        bl_col = col_of(bc[CHUNK - 1:CHUNK, :], pick_f)
        g_end = bl_col - b_col + ig_col
        seg_max = jnp.concatenate(
            [jnp.broadcast_to(jnp.max(g_end[h * CHUNK:(h + 1) * CHUNK, :], axis=0, keepdims=True), (CHUNK, 1))
             for h in range(N_HEADS)], axis=0)
        m_new = jnp.maximum(bl_col + m_col, seg_max)
        pre.append(dict(rs=rs, b_col=b_col, m_col=m_col, wk=jnp.exp(g_end - m_new),
                        cs_col=jnp.exp(bl_col + m_col - m_new),
                        b_row=b_rows[c:c + 1, :], ig_row=ig_rows[c:c + 1, :]))
        m_col = m_new
    m_ref[...] = m_col

    for ch in pre:
        rs = ch["rs"]
        dmat = jnp.where(causal, ch["b_col"] - ch["b_row"] + ch["ig_row"], -jnp.inf)
        inter = ch["b_col"] + ch["m_col"]
        mt = jnp.maximum(inter, jnp.max(dmat, axis=-1, keepdims=True))
        dw = jnp.exp(dmat - mt)
        iw = jnp.exp(inter - mt)
        q_f = jnp.where(same_head, tile4(q[rs, :]), 0.0)
        k_f = jnp.where(same_head, tile4(k[rs, :]), 0.0)
        q_bd = q_f.astype(BF16)
        v_bd = jnp.where(same_head, tile4(v[rs, :]), 0.0).astype(BF16)
        sqk = _dot_nt(q_bd, tile4(k[rs, :].astype(BF16))) * dw
        cmat = c_ref[...]
        nvec = n_ref[...]
        num = iw * _dot(q_bd, cmat.astype(BF16)) + _dot(sqk.astype(BF16), v_bd)
        den = (iw * jnp.sum(q_f * nvec, axis=-1, keepdims=True)
               + jnp.sum(sqk, axis=-1, keepdims=True))
        hh = num / jnp.maximum(jnp.abs(den), jnp.exp(-mt))
        kw = k_f * ch["wk"]
        c_ref[...] = ch["cs_col"] * cmat + _dot_tn(kw.astype(BF16), v_bd)
        n_ref[...] = to_row(ch["cs_col"]) * nvec + jnp.sum(kw, axis=0, keepdims=True)
        hn = hh * lax.rsqrt(jnp.sum(hh * hh, axis=-1, keepdims=True) * (1.0 / HEAD_DIM) + M_NORM_EPS)
        hc = (hn[0:CHUNK, :] + hn[CHUNK:2 * CHUNK, :] + hn[2 * CHUNK:3 * CHUNK, :] + hn[3 * CHUNK:, :])
        o_ref[rs, :] = (hc * ng * o_gate[rs, :]).astype(o_ref.dtype)


def _mlstm(proj, gates_r, conv_w, conv_b, b_i, b_f, norm_g, batch, seq, tb=512):
    nblk = seq // tb
    bias = jnp.concatenate([b_i, b_f]).astype(F32)
    bias_lanes = jnp.pad(bias, (0, 128 - 2 * N_HEADS))[None, :]
    bias_r = jnp.repeat(bias, CHUNK)[None, :]
    kern = functools.partial(_mlstm_kernel, tb=tb)
    const = lambda b, i: (0, 0)
    return pl.pallas_call(
        kern,
        grid=(batch, nblk),
        in_specs=[
            pl.BlockSpec((tb, 4 * M_WIDTH), lambda b, i: (b * nblk + i, 1)),
            pl.BlockSpec((tb, 128), lambda b, i: (b * nblk + i, GATE_COL_BLOCK)),
            pl.BlockSpec((tb // CHUNK, 2 * N_HEADS * CHUNK), lambda b, i: (b * nblk + i, 0)),
            pl.BlockSpec((4, 2 * M_WIDTH), const),
            pl.BlockSpec((1, 2 * M_WIDTH), const),
            pl.BlockSpec((1, 128), const),
            pl.BlockSpec((1, 2 * N_HEADS * CHUNK), const),
            pl.BlockSpec((1, M_WIDTH), const),
        ],
        out_specs=pl.BlockSpec((tb, M_WIDTH), lambda b, i: (b * nblk + i, 0)),
        out_shape=jax.ShapeDtypeStruct((batch * seq, M_WIDTH), BF16),
        scratch_shapes=[
            pltpu.VMEM((8, 2 * M_WIDTH), F32),
            pltpu.VMEM((M_WIDTH, M_WIDTH), F32),
            pltpu.VMEM((1, M_WIDTH), F32),
            pltpu.VMEM((N_HEADS * CHUNK, 1), F32),
        ],
        compiler_params=pltpu.CompilerParams(
            dimension_semantics=("parallel", "arbitrary"), vmem_limit_bytes=VMEM_LIMIT),
        name="mlstm",
    )(proj, proj, gates_r, conv_w, conv_b[None, :], bias_lanes, bias_r, norm_g[None, :])


def _rwkv_kernel(p_ref, mu_ref, wl_ref, w0_ref, a0_ref, kk_ref, ka_ref, rk_ref, lng_ref, lnb_ref,
                 o_ref, carry_ref, s_ref, y_ref, *, tb):
    @pl.when(pl.program_id(1) == 0)
    def _():
        carry_ref[...] = jnp.zeros_like(carry_ref)
        s_ref[...] = jnp.zeros_like(s_ref)

    p = p_ref[...]
    ext = jnp.concatenate([carry_ref[...], p], axis=0)
    carry_ref[...] = p[tb - 8:, :]
    prev = pltpu.roll(ext, 1, axis=0)[8:, :]
    pm = p + (prev - p) * mu_ref[...]
    r = pm[:, :R_WIDTH]
    k = pm[:, R_WIDTH:2 * R_WIDTH]
    v = pm[:, 2 * R_WIDTH:3 * R_WIDTH]
    lo = pm[:, 3 * R_WIDTH:]
    lane = _iota2(lo.shape, 1)
    act = jnp.where(lane < 32, jnp.tanh(lo), jnp.where(lane < 64, lo, _sigmoid(lo)))
    lora = _dot_x3(act, wl_ref[...])
    w_log = _log_sigmoid(w0_ref[...] + lora[:, :R_WIDTH]) - 0.5
    lw = -jnp.exp(w_log)
    a = _sigmoid(a0_ref[...] + lora[:, R_WIDTH:2 * R_WIDTH])
    gate = lora[:, 2 * R_WIDTH:]

    n = N_HEADS * CHUNK
    row = _iota2((n, n), 0)
    col = _iota2((n, n), 1)
    same_head = (row >> 6) == (col >> 6)
    t_row = row & (CHUNK - 1)
    t_col = col & (CHUNK - 1)
    incl = same_head & (t_col <= t_row)
    strict = same_head & (t_col < t_row)
    eye = row == col
    eye_f = eye.astype(F32)
    head_ones = same_head.astype(BF16)
    n_levels = int(math.log2(CHUNK))
    level_masks = [
        same_head & ((t_row >> (l + 1)) == (t_col >> (l + 1)))
        & (((t_row >> l) & 1) == 1) & (((t_col >> l) & 1) == 0)
        for l in range(n_levels)
    ]
    tril64 = (_iota2((CHUNK, CHUNK), 1) <= _iota2((CHUNK, CHUNK), 0)).astype(BF16)

    kk = k * kk_ref[...]
    kk = kk / jnp.maximum(jnp.sqrt(_dot_f32_lhs(kk * kk, head_ones)), 1e-12)
    k2 = k * (1.0 + (a - 1.0) * ka_ref[...])
    bvec = kk * a

    def tile4(z):
        return jnp.concatenate([z, z, z, z], axis=0)

    def block_diag(z):
        return jnp.where(same_head, tile4(z), 0.0).astype(BF16)

    chunks = []
    for c in range(tb // CHUNK):
        rs = slice(c * CHUNK, (c + 1) * CHUNK)
        lwc = lw[rs, :]
        g = _dot_f32_rhs(tril64, lwc)
        g_last = g[CHUNK - 1:CHUNK, :]
        e_pos = jnp.exp(g)
        e_neg = jnp.exp(-g)
        e_end = jnp.exp(g_last - g)
        e_last = jnp.exp(g_last)
        ar = jnp.concatenate([block_diag(-kk[rs, :] * jnp.exp(g - lwc)),
                              block_diag(r[rs, :] * e_pos)], axis=0)
        bk = jnp.concatenate([tile4((bvec[rs, :] * e_neg).astype(BF16)),
                              tile4((k2[rs, :] * e_neg).astype(BF16))], axis=0)
        bkh = jnp.concatenate([tile4((bvec[rs, :] * e_end).astype(BF16)),
                               tile4((k2[rs, :] * e_end).astype(BF16))], axis=0)
        v_bd = block_diag(v[rs, :])
        mm = _dot_nt(ar, bk)
        ab = jnp.where(strict, mm[:n, :n], 0.0)
        ak = jnp.where(strict, mm[:n, n:], 0.0).astype(BF16)
        rbk = jnp.concatenate([jnp.where(incl, mm[n:, :n], 0.0).astype(BF16),
                               jnp.where(incl, mm[n:, n:], 0.0).astype(BF16)], axis=1)
        e_last_col = jnp.sum(jnp.where(eye, e_last, 0.0), axis=-1, keepdims=True)
        chunks.append(dict(rs=rs, ar=ar, bkh=bkh, v_bd=v_bd, ab=ab, rbk=rbk, akv=_dot(ak, v_bd),
                           e_last_col=e_last_col, tinv=eye_f + jnp.where(level_masks[0], ab, 0.0)))
    for l in range(1, n_levels):
        for ch in chunks:
            t_bf = ch["tinv"].astype(BF16)
            join = jnp.where(level_masks[l], ch["ab"], 0.0).astype(BF16)
            ch["tinv"] = ch["tinv"] + _dot(t_bf, _dot(join, t_bf).astype(BF16))
    for ch in chunks:
        h0 = s_ref[...]
        ah = _dot(ch["ar"], h0.astype(BF16))
        u = _dot(ch["tinv"].astype(BF16), (ah[:n, :] + ch["akv"]).astype(BF16))
        uv = jnp.concatenate([u.astype(BF16), ch["v_bd"]], axis=0)
        o = ah[n:, :] + _dot(ch["rbk"], uv)
        y_ref[ch["rs"], :] = (o[0:CHUNK, :] + o[CHUNK:2 * CHUNK, :]
                              + o[2 * CHUNK:3 * CHUNK, :] + o[3 * CHUNK:, :])
        s_ref[...] = h0 * ch["e_last_col"] + jnp.where(same_head, _dot_tn(ch["bkh"], uv), 0.0)

    y = y_ref[...]
    inv_d = 1.0 / HEAD_DIM
    mean = _dot_f32_lhs(y, head_ones) * inv_d
    yc = y - mean
    var = _dot_f32_lhs(yc * yc, head_ones) * inv_d
    yn = yc * lax.rsqrt(var + R_LN_EPS) * lng_ref[...] + lnb_ref[...]
    bonus = _dot_f32_lhs(r * k2 * rk_ref[...], head_ones)
    o_ref[...] = ((yn + bonus * v) * gate).astype(o_ref.dtype)


def _rwkv(proj, mu, w0, w_up, a0, a_up, g_up, k_k, k_a, r_k, ln_g, ln_b, batch, seq, tb=512):
    nblk = seq // tb
    w_lora = jnp.zeros((128, 3 * R_WIDTH), F32)
    w_lora = w_lora.at[0:32, 0:R_WIDTH].set(w_up)
    w_lora = w_lora.at[32:64, R_WIDTH:2 * R_WIDTH].set(a_up)
    w_lora = w_lora.at[64:128, 2 * R_WIDTH:].set(g_up)
    kern = functools.partial(_rwkv_kernel, tb=tb)
    const = lambda b, i: (0, 0)
    vec = pl.BlockSpec((1, R_WIDTH), const)
    return pl.pallas_call(
        kern,
        grid=(batch, nblk),
        in_specs=[
            pl.BlockSpec((tb, R_PROJ), lambda b, i: (b * nblk + i, 0)),
            pl.BlockSpec((1, R_PROJ), const),
            pl.BlockSpec((128, 3 * R_WIDTH), const),
            vec, vec, vec, vec, vec, vec, vec,
        ],
        out_specs=pl.BlockSpec((tb, R_WIDTH), lambda b, i: (b * nblk + i, 0)),
        out_shape=jax.ShapeDtypeStruct((batch * seq, R_WIDTH), BF16),
        scratch_shapes=[
            pltpu.VMEM((8, R_PROJ), F32),
            pltpu.VMEM((R_WIDTH, R_WIDTH), F32),
            pltpu.VMEM((tb, R_WIDTH), F32),
        ],
        compiler_params=pltpu.CompilerParams(
            dimension_semantics=("parallel", "arbitrary"), vmem_limit_bytes=VMEM_LIMIT),
        name="rwkv7",
    )(proj, mu[None, :], w_lora, w0[None, :], a0[None, :], k_k[None, :], k_a[None, :],
      r_k.reshape(1, R_WIDTH), ln_g[None, :], ln_b[None, :])


def _attn_kernel(lq1_ref, lk1_ref, lq2_ref, lk2_ref, g_ref, q_ref, k_ref, v_ref, o_ref,
                 vt_ref, sa_ref, sb_ref, p_ref, m_ref, l_ref, acc_ref, *, tq, lam_init):
    qi = pl.program_id(2)

    @pl.when(qi == 0)
    def _():
        for jj in range(vt_ref.shape[0]):
            vt_ref[jj] = v_ref[jj * tq:(jj + 1) * tq, :].astype(F32).T.astype(BF16)

    lam = (jnp.exp(jnp.sum(lq1_ref[...] * lk1_ref[...], axis=-1, keepdims=True))
           - jnp.exp(jnp.sum(lq2_ref[...] * lk2_ref[...], axis=-1, keepdims=True)) + lam_init)
    q = q_ref[...]
    lane = _iota2(q.shape, 1)
    zero = jnp.zeros_like(q)
    q_halves = (jnp.where(lane < HEAD_DIM, q, zero), jnp.where(lane >= HEAD_DIM, q, zero))
    strips = [slice(r * ATTN_STRIP, (r + 1) * ATTN_STRIP) for r in range(tq // ATTN_STRIP)]
    groups = ATTN_STRIP // 8

    def put_scores(j, dst_ref):
        start = pl.multiple_of(j * tq, tq)
        kb = k_ref[pl.ds(start, tq), :]
        for i in range(2):
            dst_ref[i] = _dot_nt(kb, q_halves[i])

    def strip_scores(src_ref, i, rows, masked):
        s = src_ref[i, rows, :]
        if masked:
            keep = _iota2((ATTN_STRIP, tq), 0) + rows.start <= _iota2((ATTN_STRIP, tq), 1)
            s = jnp.where(keep, s, -jnp.inf)
        return s.reshape(groups, 8, tq)

    def update(j, src_ref, masked):
        vtb = vt_ref[j]
        for i in range(2):
            m = m_ref[i]
            mx8 = None
            for rows in strips:
                t = jnp.max(strip_scores(src_ref, i, rows, masked), axis=0)
                mx8 = t if mx8 is None else jnp.maximum(mx8, t)
            m_new = jnp.maximum(m, jnp.max(mx8, axis=0, keepdims=True))
            sum8 = jnp.zeros((8, tq), F32)
            for rows in strips:
                pr = jnp.exp(strip_scores(src_ref, i, rows, masked) - m_new)
                sum8 = sum8 + jnp.sum(pr, axis=0)
                p_ref[i, rows, :] = pr.reshape(ATTN_STRIP, tq).astype(BF16)
            alpha = jnp.exp(m - m_new)
            m_ref[i] = m_new
            l_ref[i] = alpha * l_ref[i] + jnp.sum(sum8, axis=0, keepdims=True)
            acc_ref[i] = alpha * acc_ref[i] + _dot(vtb, p_ref[i])

    m_ref[...] = jnp.full(m_ref.shape, -1e30, F32)
    l_ref[...] = jnp.zeros_like(l_ref)
    acc_ref[...] = jnp.zeros_like(acc_ref)

    def body(t, carry):
        j = 2 * t
        put_scores(j + 1, sb_ref)
        update(j, sa_ref, False)
        put_scores(j + 2, sa_ref)
        update(j + 1, sb_ref, False)
        return carry

    put_scores(0, sa_ref)
    lax.fori_loop(0, qi // 2, body, 0)

    @pl.when(qi % 2 == 1)
    def _():
        put_scores(qi, sb_ref)
        update(qi - 1, sa_ref, False)
        update(qi, sb_ref, True)

    @pl.when(qi % 2 == 0)
    def _():
        update(qi, sa_ref, True)

    o = (acc_ref[0] / l_ref[0] - lam * (acc_ref[1] / l_ref[1])).T
    o = o * lax.rsqrt(jnp.mean(o * o, axis=-1, keepdims=True) + A_NORM_EPS) * g_ref[...]
    o_ref[...] = (o * (1.0 - lam_init)).astype(o_ref.dtype)


def _diff_attn(qkv, lq1, lk1, lq2, lk2, norm_g, lam_init, batch, seq, tq=512):
    nq = seq // tq
    kern = functools.partial(_attn_kernel, tq=tq, lam_init=lam_init)
    const = lambda b, h, i: (0, 0)
    lvec = pl.BlockSpec((1, HEAD_DIM), const)
    return pl.pallas_call(
        kern,
        grid=(batch, N_HEADS, nq),
        in_specs=[
            lvec, lvec, lvec, lvec,
            pl.BlockSpec((1, 128), lambda b, h, i: (0, h)),
            pl.BlockSpec((tq, 128), lambda b, h, i: (b * nq + i, h)),
            pl.BlockSpec((seq, 128), lambda b, h, i: (b, N_HEADS + h)),
            pl.BlockSpec((seq, 128), lambda b, h, i: (b, 2 * N_HEADS + h)),
        ],
        out_specs=pl.BlockSpec((tq, 128), lambda b, h, i: (b * nq + i, h)),
        out_shape=jax.ShapeDtypeStruct((batch * seq, A_WIDTH), BF16),
        scratch_shapes=[
            pltpu.VMEM((nq, 2 * HEAD_DIM, tq), BF16),
            pltpu.VMEM((2, tq, tq), F32),
            pltpu.VMEM((2, tq, tq), F32),
            pltpu.VMEM((2, tq, tq), BF16),
            pltpu.VMEM((2, 1, tq), F32),
            pltpu.VMEM((2, 1, tq), F32),
            pltpu.VMEM((2, 2 * HEAD_DIM, tq), F32),
        ],
        compiler_params=pltpu.CompilerParams(
            dimension_semantics=("parallel", "parallel", "arbitrary"), vmem_limit_bytes=VMEM_LIMIT),
        name="diff_attn",
    )(lq1[None, :], lk1[None, :], lq2[None, :], lk2[None, :], norm_g[None, :], qkv, qkv, qkv)


def _out_ffn_kernel(x_ref, ym_ref, yr_ref, ya_ref, wo_ref, g2_ref, wu_ref, wd_ref, gf_ref, o_ref,
                    h_ref, acc_ref, *, final_norm):
    j = pl.program_id(1)

    @pl.when(j == 0)
    def _():
        xn = (x_ref[...]
              + _dot(ym_ref[...], wo_ref[0:M_WIDTH, :])
              + _dot(yr_ref[...], wo_ref[M_WIDTH:M_WIDTH + R_WIDTH, :])
              + _dot(ya_ref[...], wo_ref[M_WIDTH + R_WIDTH:, :]))
        acc_ref[...] = xn
        hn = xn * lax.rsqrt(jnp.mean(xn * xn, axis=-1, keepdims=True) + NORM_EPS) * g2_ref[...]
        h_ref[...] = hn.astype(BF16)

    u = jnp.maximum(_dot(h_ref[...], wu_ref[...]), 0.0)
    acc_ref[...] += _dot((u * u).astype(BF16), wd_ref[...])

    @pl.when(j == pl.num_programs(1) - 1)
    def _():
        y = acc_ref[...]
        if final_norm:
            y = y * lax.rsqrt(jnp.mean(y * y, axis=-1, keepdims=True) + NORM_EPS) * gf_ref[...]
        o_ref[...] = y


def _out_ffn(x, ym, yr, ya, w_out, g2, w_up, w_down, g_final, final_norm, tm=1024, tf=512):
    t, d = x.shape
    f = w_up.shape[1]
    kern = functools.partial(_out_ffn_kernel, final_norm=final_norm)
    rows = lambda i, j: (i, 0)
    const = lambda i, j: (0, 0)
    return pl.pallas_call(
        kern,
        grid=(t // tm, f // tf),
        in_specs=[
            pl.BlockSpec((tm, d), rows),
            pl.BlockSpec((tm, M_WIDTH), rows),
            pl.BlockSpec((tm, R_WIDTH), rows),
            pl.BlockSpec((tm, A_WIDTH), rows),
            pl.BlockSpec((d, d), const),
            pl.BlockSpec((1, d), const),
            pl.BlockSpec((d, tf), lambda i, j: (0, j)),
            pl.BlockSpec((tf, d), lambda i, j: (j, 0)),
            pl.BlockSpec((1, d), const),
        ],
        out_specs=pl.BlockSpec((tm, d), rows),
        out_shape=jax.ShapeDtypeStruct((t, d), F32),
        scratch_shapes=[pltpu.VMEM((tm, d), BF16), pltpu.VMEM((tm, d), F32)],
        compiler_params=pltpu.CompilerParams(
            dimension_semantics=("parallel", "arbitrary"), vmem_limit_bytes=VMEM_LIMIT),
        name="out_ffn",
    )(x, ym, yr, ya, w_out, g2, w_up, w_down, g_final)


def _split_in_proj(w):
    m_main = w[:, 0:4 * M_WIDTH]
    m_gate = w[:, 4 * M_WIDTH:4 * M_WIDTH + 2 * N_HEADS]
    r0 = 4 * M_WIDTH + 2 * N_HEADS
    r_all = w[:, r0:r0 + R_PROJ]
    a0 = r0 + R_PROJ
    a_q = w[:, a0:a0 + A_WIDTH] * (HEAD_DIM ** -0.5)
    a_kv = w[:, a0 + A_WIDTH:a0 + A_PROJ]
    pad = jnp.zeros((w.shape[0], 128 - 2 * N_HEADS), w.dtype)
    w_main = jnp.concatenate([r_all, m_gate, pad, m_main], axis=1).astype(BF16)
    w_attn = jnp.concatenate([a_q, a_kv], axis=1).astype(BF16)
    return w_main, w_attn


def kernel(x, norm1_g, w_in, m_conv_w, m_conv_b, m_b_i, m_b_f, m_norm_g, r_mu, r_w0, r_w_up, r_a0, r_a_up, r_g_up, r_k_k, r_k_a, r_r_k, r_ln_g, r_ln_b, a_lq1, a_lk1, a_lq2, a_lk2, a_norm_g, w_out, norm2_g, w_ff_up, w_ff_down, final_g):
    batch, seq, d = x.shape
    depth = w_in.shape[0]
    xt = x.reshape(batch * seq, d)
    for l in range(depth):
        w_main, w_attn = _split_in_proj(w_in[l])
        g1 = norm1_g[l][None, :]
        proj, qkv = _in_proj(xt, g1, w_main, w_attn)
        gates_r = (proj[:, R_PROJ:R_PROJ + 2 * N_HEADS]
                   .reshape(batch * seq // CHUNK, CHUNK, 2 * N_HEADS).transpose(0, 2, 1)
                   .reshape(batch * seq // CHUNK, 2 * N_HEADS * CHUNK))
        y_m = _mlstm(proj, gates_r, m_conv_w[l], m_conv_b[l], m_b_i[l], m_b_f[l], m_norm_g[l], batch, seq)
        y_r = _rwkv(proj, r_mu[l], r_w0[l], r_w_up[l], r_a0[l], r_a_up[l], r_g_up[l],
                    r_k_k[l], r_k_a[l], r_r_k[l], r_ln_g[l], r_ln_b[l], batch, seq)
        lam_init = 0.8 - 0.6 * math.exp(-0.3 * l)
        y_a = _diff_attn(qkv, a_lq1[l], a_lk1[l], a_lq2[l], a_lk2[l], a_norm_g[l], lam_init, batch, seq)
        xt = _out_ffn(xt, y_m, y_r, y_a, w_out[l].astype(BF16), norm2_g[l][None, :],
                      w_ff_up[l].astype(BF16), w_ff_down[l].astype(BF16), final_g[None, :],
                      final_norm=(l == depth - 1))
    return xt.reshape(batch, seq, d)
```

```python
import functools
import math

import jax
import jax.numpy as jnp
from jax import lax
from jax.experimental import pallas as pl
from jax.experimental.pallas import tpu as pltpu

F32 = jnp.float32
BF16 = jnp.bfloat16
HI = lax.Precision.HIGHEST

D_MODEL = 1024
HEAD_DIM = 64
N_HEADS = 4
M_WIDTH = 256
R_WIDTH = 256
R_PROJ = 896
A_WIDTH = 512
A_PROJ = 1536
D_FF = 4096
CHUNK = 64
ATTN_STRIP = 64
RWKV_TURNS_PER_MLSTM_TURN = 6
ATTN_SUM_ROWS = 16
NORM_EPS = 1e-6
M_NORM_EPS = 1e-6
R_LN_EPS = 64e-5
A_NORM_EPS = 1e-5
MAIN_PROJ = 2048
GATE_COL_BLOCK = 7
VMEM_LIMIT = 52 * 1024 * 1024


def _dot(a, b, prec=None):
    return jnp.dot(a, b, preferred_element_type=F32, precision=prec)


def _dot_nt(a, b, prec=None):
    return lax.dot_general(a, b, (((1,), (1,)), ((), ())), preferred_element_type=F32, precision=prec)


def _dot_tn(a, b, prec=None):
    return lax.dot_general(a, b, (((0,), (0,)), ((), ())), preferred_element_type=F32, precision=prec)


def _split3(x):
    hi = x.astype(BF16)
    rem = x - hi.astype(F32)
    mid = rem.astype(BF16)
    lo = (rem - mid.astype(F32)).astype(BF16)
    return hi, mid, lo


def _dot_f32_lhs(a, b_bf16):
    hi, mid, lo = _split3(a)
    return _dot(hi, b_bf16) + _dot(mid, b_bf16) + _dot(lo, b_bf16)


def _dot_f32_rhs(a_bf16, b):
    hi, mid, lo = _split3(b)
    return _dot(a_bf16, hi) + _dot(a_bf16, mid) + _dot(a_bf16, lo)


def _dot_x3(a, b):
    ah = a.astype(BF16)
    bh = b.astype(BF16)
    al = (a - ah.astype(F32)).astype(BF16)
    bl = (b - bh.astype(F32)).astype(BF16)
    return _dot(ah, bh) + _dot(ah, bl) + _dot(al, bh)


def _sigmoid(x):
    return 1.0 / (1.0 + jnp.exp(-x))


def _log_sigmoid(x):
    return jnp.minimum(x, 0.0) - jnp.log(1.0 + jnp.exp(-jnp.abs(x)))


def _iota2(shape, dim):
    return lax.broadcasted_iota(jnp.int32, shape, dim)


def _in_proj_kernel(x_ref, g_ref, wm_ref, wa_ref, om_ref, oa_ref):
    x = x_ref[...]
    h = (x * lax.rsqrt(jnp.mean(x * x, axis=-1, keepdims=True) + NORM_EPS) * g_ref[...]).astype(BF16)
    om_ref[...] = _dot(h, wm_ref[...])
    oa_ref[...] = _dot(h, wa_ref[...]).astype(oa_ref.dtype)


def _in_proj(x, g, w_main, w_attn, tm=512):
    t, d = x.shape
    rows = lambda i: (i, 0)
    const = lambda i: (0, 0)
    return pl.pallas_call(
        _in_proj_kernel,
        grid=(t // tm,),
        in_specs=[
            pl.BlockSpec((tm, d), rows),
            pl.BlockSpec((1, d), const),
            pl.BlockSpec((d, MAIN_PROJ), const),
            pl.BlockSpec((d, A_PROJ), const),
        ],
        out_specs=[pl.BlockSpec((tm, MAIN_PROJ), rows), pl.BlockSpec((tm, A_PROJ), rows)],
        out_shape=[jax.ShapeDtypeStruct((t, MAIN_PROJ), F32), jax.ShapeDtypeStruct((t, A_PROJ), BF16)],
        compiler_params=pltpu.CompilerParams(
            dimension_semantics=("parallel",), vmem_limit_bytes=VMEM_LIMIT),
        name="in_proj",
    )(x, g, w_main, w_attn)


def _mlstm_kernel(x_ref, gc_ref, gr_ref, cw_ref, cb_ref, bc_ref, br_ref, ng_ref, o_ref,
                  carry_ref, c_ref, n_ref, m_ref, *, tb):
    @pl.when(pl.program_id(1) == 0)
    def _():
        carry_ref[...] = jnp.zeros_like(carry_ref)
        c_ref[...] = jnp.zeros_like(c_ref)
        n_ref[...] = jnp.zeros_like(n_ref)
        m_ref[...] = jnp.zeros_like(m_ref)

    x = x_ref[...]
    qk_pre = x[:, :2 * M_WIDTH]
    ext = jnp.concatenate([carry_ref[...], qk_pre], axis=0)
    carry_ref[...] = qk_pre[tb - 8:, :]
    cw = cw_ref[...]
    acc = cb_ref[...] + cw[3:4, :] * qk_pre
    for j in (1, 2, 3):
        acc = acc + cw[3 - j:4 - j, :] * pltpu.roll(ext, j, axis=0)[8:, :]
    qk = acc * _sigmoid(acc)
    q = qk[:, :M_WIDTH]
    k = qk[:, M_WIDTH:] * (HEAD_DIM ** -0.5)
    v = x[:, 2 * M_WIDTH:3 * M_WIDTH]
    o_gate = _sigmoid(x[:, 3 * M_WIDTH:])

    n = N_HEADS * CHUNK
    row = _iota2((n, n), 0)
    col = _iota2((n, n), 1)
    same_head = (row >> 6) == (col >> 6)
    causal = same_head & ((col & (CHUNK - 1)) <= (row & (CHUNK - 1)))
    eye = row == col
    triu_bd = (same_head & ((row & (CHUNK - 1)) <= (col & (CHUNK - 1)))).astype(BF16)
    tril64 = (_iota2((CHUNK, CHUNK), 1) <= _iota2((CHUNK, CHUNK), 0)).astype(BF16)
    lane_g = _iota2((n, 128), 1)
    head_g = _iota2((n, 128), 0) >> 6
    pick_i = lane_g == head_g
    pick_f = lane_g == head_g + N_HEADS
    ng = ng_ref[...]

    def tile4(z):
        return jnp.concatenate([z, z, z, z], axis=0)

    def col_of(z, pick):
        zz = tile4(z) if z.shape[0] == CHUNK else z
        return jnp.sum(jnp.where(pick, zz, 0.0), axis=-1, keepdims=True)

    def to_row(z_col):
        return jnp.sum(jnp.where(eye, z_col, 0.0), axis=0, keepdims=True)

    gc = gc_ref[...] + bc_ref[...]
    gc = jnp.where(_iota2(gc.shape, 1) < N_HEADS, gc, _log_sigmoid(gc))
    gr = gr_ref[...] + br_ref[...]
    ig_rows = gr[:, :n]
    b_rows = _dot_f32_lhs(_log_sigmoid(gr[:, n:]), triu_bd)

    m_col = m_ref[...]
    pre = []
    for c in range(tb // CHUNK):
        rs = slice(c * CHUNK, (c + 1) * CHUNK)
        gcc = gc[rs, :]
        bc = _dot_f32_rhs(tril64, gcc)
        b_col = col_of(bc, pick_f)
        ig_col = col_of(gcc, pick_i)
        bl_col = col_of(bc[CHUNK - 1:CHUNK, :], pick_f)
        g_end = bl_col - b_col + ig_col
        seg_max = jnp.concatenate(
            [jnp.broadcast_to(jnp.max(g_end[h * CHUNK:(h + 1) * CHUNK, :], axis=0, keepdims=True), (CHUNK, 1))
             for h in range(N_HEADS)], axis=0)
        m_new = jnp.maximum(bl_col + m_col, seg_max)
        pre.append(dict(rs=rs, b_col=b_col, m_col=m_col, wk=jnp.exp(g_end - m_new),
                        cs_col=jnp.exp(bl_col + m_col - m_new),
                        b_row=b_rows[c:c + 1, :], ig_row=ig_rows[c:c + 1, :]))
        m_col = m_new
    m_ref[...] = m_col
    yield

    for ch in pre:
        rs = ch["rs"]
        dmat = jnp.where(causal, ch["b_col"] - ch["b_row"] + ch["ig_row"], -jnp.inf)
        inter = ch["b_col"] + ch["m_col"]
        mt = jnp.maximum(inter, jnp.max(dmat, axis=-1, keepdims=True))
        dw = jnp.exp(dmat - mt)
        iw = jnp.exp(inter - mt)
        q_f = jnp.where(same_head, tile4(q[rs, :]), 0.0)
        k_f = jnp.where(same_head, tile4(k[rs, :]), 0.0)
        q_bd = q_f.astype(BF16)
        v_bd = jnp.where(same_head, tile4(v[rs, :]), 0.0).astype(BF16)
        sqk = _dot_nt(q_bd, tile4(k[rs, :].astype(BF16))) * dw
        cmat = c_ref[...]
        nvec = n_ref[...]
        num = iw * _dot(q_bd, cmat.astype(BF16)) + _dot(sqk.astype(BF16), v_bd)
        den = (iw * jnp.sum(q_f * nvec, axis=-1, keepdims=True)
               + jnp.sum(sqk, axis=-1, keepdims=True))
        hh = num / jnp.maximum(jnp.abs(den), jnp.exp(-mt))
        kw = k_f * ch["wk"]
        c_ref[...] = ch["cs_col"] * cmat + _dot_tn(kw.astype(BF16), v_bd)
        n_ref[...] = to_row(ch["cs_col"]) * nvec + jnp.sum(kw, axis=0, keepdims=True)
        hn = hh * lax.rsqrt(jnp.sum(hh * hh, axis=-1, keepdims=True) * (1.0 / HEAD_DIM) + M_NORM_EPS)
        hc = (hn[0:CHUNK, :] + hn[CHUNK:2 * CHUNK, :] + hn[2 * CHUNK:3 * CHUNK, :] + hn[3 * CHUNK:, :])
        o_ref[rs, :] = (hc * ng * o_gate[rs, :]).astype(o_ref.dtype)
        yield


def _mlstm_parts(proj, gates_r, conv_w, conv_b, b_i, b_f, norm_g, batch, seq, tb):
    nblk = seq // tb
    bias = jnp.concatenate([b_i, b_f]).astype(F32)
    bias_lanes = jnp.pad(bias, (0, 128 - 2 * N_HEADS))[None, :]
    bias_r = jnp.repeat(bias, CHUNK)[None, :]
    const = lambda b, i: (0, 0)
    return dict(
        args=(proj, proj, gates_r, conv_w, conv_b[None, :], bias_lanes, bias_r, norm_g[None, :]),
        in_specs=[
            pl.BlockSpec((tb, 4 * M_WIDTH), lambda b, i: (b * nblk + i, 1)),
            pl.BlockSpec((tb, 128), lambda b, i: (b * nblk + i, GATE_COL_BLOCK)),
            pl.BlockSpec((tb // CHUNK, 2 * N_HEADS * CHUNK), lambda b, i: (b * nblk + i, 0)),
            pl.BlockSpec((4, 2 * M_WIDTH), const),
            pl.BlockSpec((1, 2 * M_WIDTH), const),
            pl.BlockSpec((1, 128), const),
            pl.BlockSpec((1, 2 * N_HEADS * CHUNK), const),
            pl.BlockSpec((1, M_WIDTH), const),
        ],
        out_spec=pl.BlockSpec((tb, M_WIDTH), lambda b, i: (b * nblk + i, 0)),
        out_shape=jax.ShapeDtypeStruct((batch * seq, M_WIDTH), BF16),
        scratch=[
            pltpu.VMEM((8, 2 * M_WIDTH), F32),
            pltpu.VMEM((M_WIDTH, M_WIDTH), F32),
            pltpu.VMEM((1, M_WIDTH), F32),
            pltpu.VMEM((N_HEADS * CHUNK, 1), F32),
        ])


def _rwkv_kernel(p_ref, mu_ref, wl_ref, w0_ref, a0_ref, kk_ref, ka_ref, rk_ref, lng_ref, lnb_ref,
                 o_ref, carry_ref, s_ref, y_ref, *, tb):
    @pl.when(pl.program_id(1) == 0)
    def _():
        carry_ref[...] = jnp.zeros_like(carry_ref)
        s_ref[...] = jnp.zeros_like(s_ref)

    p = p_ref[...]
    ext = jnp.concatenate([carry_ref[...], p], axis=0)
    carry_ref[...] = p[tb - 8:, :]
    prev = pltpu.roll(ext, 1, axis=0)[8:, :]
    pm = p + (prev - p) * mu_ref[...]
    r = pm[:, :R_WIDTH]
    k = pm[:, R_WIDTH:2 * R_WIDTH]
    v = pm[:, 2 * R_WIDTH:3 * R_WIDTH]
    lo = pm[:, 3 * R_WIDTH:]
    lane = _iota2(lo.shape, 1)
    act = jnp.where(lane < 32, jnp.tanh(lo), jnp.where(lane < 64, lo, _sigmoid(lo)))
    lora = _dot_x3(act, wl_ref[...])
    w_log = _log_sigmoid(w0_ref[...] + lora[:, :R_WIDTH]) - 0.5
    lw = -jnp.exp(w_log)
    a = _sigmoid(a0_ref[...] + lora[:, R_WIDTH:2 * R_WIDTH])
    gate = lora[:, 2 * R_WIDTH:]

    n = N_HEADS * CHUNK
    row = _iota2((n, n), 0)
    col = _iota2((n, n), 1)
    same_head = (row >> 6) == (col >> 6)
    t_row = row & (CHUNK - 1)
    t_col = col & (CHUNK - 1)
    incl = same_head & (t_col <= t_row)
    strict = same_head & (t_col < t_row)
    eye = row == col
    eye_f = eye.astype(F32)
    head_ones = same_head.astype(BF16)
    n_levels = int(math.log2(CHUNK))
    level_masks = [
        same_head & ((t_row >> (l + 1)) == (t_col >> (l + 1)))
        & (((t_row >> l) & 1) == 1) & (((t_col >> l) & 1) == 0)
        for l in range(n_levels)
    ]
    tril64 = (_iota2((CHUNK, CHUNK), 1) <= _iota2((CHUNK, CHUNK), 0)).astype(BF16)

    kk = k * kk_ref[...]
    kk = kk / jnp.maximum(jnp.sqrt(_dot_f32_lhs(kk * kk, head_ones)), 1e-12)
    k2 = k * (1.0 + (a - 1.0) * ka_ref[...])
    bvec = kk * a

    def tile4(z):
        return jnp.concatenate([z, z, z, z], axis=0)

    def block_diag(z):
        return jnp.where(same_head, tile4(z), 0.0).astype(BF16)

    chunks = []
    for c in range(tb // CHUNK):
        rs = slice(c * CHUNK, (c + 1) * CHUNK)
        lwc = lw[rs, :]
        g = _dot_f32_rhs(tril64, lwc)
        g_last = g[CHUNK - 1:CHUNK, :]
        e_pos = jnp.exp(g)
        e_neg = jnp.exp(-g)
        e_end = jnp.exp(g_last - g)
        e_last = jnp.exp(g_last)
        ar = jnp.concatenate([block_diag(-kk[rs, :] * jnp.exp(g - lwc)),
                              block_diag(r[rs, :] * e_pos)], axis=0)
        bk = jnp.concatenate([tile4((bvec[rs, :] * e_neg).astype(BF16)),
                              tile4((k2[rs, :] * e_neg).astype(BF16))], axis=0)
        bkh = jnp.concatenate([tile4((bvec[rs, :] * e_end).astype(BF16)),
                               tile4((k2[rs, :] * e_end).astype(BF16))], axis=0)
        v_bd = block_diag(v[rs, :])
        mm = _dot_nt(ar, bk)
        ab = jnp.where(strict, mm[:n, :n], 0.0)
        ak = jnp.where(strict, mm[:n, n:], 0.0).astype(BF16)
        rbk = jnp.concatenate([jnp.where(incl, mm[n:, :n], 0.0).astype(BF16),
                               jnp.where(incl, mm[n:, n:], 0.0).astype(BF16)], axis=1)
        e_last_col = jnp.sum(jnp.where(eye, e_last, 0.0), axis=-1, keepdims=True)
        chunks.append(dict(rs=rs, ar=ar, bkh=bkh, v_bd=v_bd, ab=ab, rbk=rbk, akv=_dot(ak, v_bd),
                           e_last_col=e_last_col, tinv=eye_f + jnp.where(level_masks[0], ab, 0.0)))
        yield
    for l in range(1, n_levels):
        for ch in chunks:
            t_bf = ch["tinv"].astype(BF16)
            join = jnp.where(level_masks[l], ch["ab"], 0.0).astype(BF16)
            ch["tinv"] = ch["tinv"] + _dot(t_bf, _dot(join, t_bf).astype(BF16))
            yield
    for ch in chunks:
        h0 = s_ref[...]
        ah = _dot(ch["ar"], h0.astype(BF16))
        u = _dot(ch["tinv"].astype(BF16), (ah[:n, :] + ch["akv"]).astype(BF16))
        uv = jnp.concatenate([u.astype(BF16), ch["v_bd"]], axis=0)
        o = ah[n:, :] + _dot(ch["rbk"], uv)
        y_ref[ch["rs"], :] = (o[0:CHUNK, :] + o[CHUNK:2 * CHUNK, :]
                              + o[2 * CHUNK:3 * CHUNK, :] + o[3 * CHUNK:, :])
        s_ref[...] = h0 * ch["e_last_col"] + jnp.where(same_head, _dot_tn(ch["bkh"], uv), 0.0)
        yield

    y = y_ref[...]
    inv_d = 1.0 / HEAD_DIM
    mean = _dot_f32_lhs(y, head_ones) * inv_d
    yc = y - mean
    var = _dot_f32_lhs(yc * yc, head_ones) * inv_d
    yn = yc * lax.rsqrt(var + R_LN_EPS) * lng_ref[...] + lnb_ref[...]
    bonus = _dot_f32_lhs(r * k2 * rk_ref[...], head_ones)
    o_ref[...] = ((yn + bonus * v) * gate).astype(o_ref.dtype)


def _rwkv_parts(proj, mu, w0, w_up, a0, a_up, g_up, k_k, k_a, r_k, ln_g, ln_b, batch, seq, tb):
    nblk = seq // tb
    w_lora = jnp.zeros((128, 3 * R_WIDTH), F32)
    w_lora = w_lora.at[0:32, 0:R_WIDTH].set(w_up)
    w_lora = w_lora.at[32:64, R_WIDTH:2 * R_WIDTH].set(a_up)
    w_lora = w_lora.at[64:128, 2 * R_WIDTH:].set(g_up)
    const = lambda b, i: (0, 0)
    vec = pl.BlockSpec((1, R_WIDTH), const)
    return dict(
        args=(proj, mu[None, :], w_lora, w0[None, :], a0[None, :], k_k[None, :], k_a[None, :],
              r_k.reshape(1, R_WIDTH), ln_g[None, :], ln_b[None, :]),
        in_specs=[
            pl.BlockSpec((tb, R_PROJ), lambda b, i: (b * nblk + i, 0)),
            pl.BlockSpec((1, R_PROJ), const),
            pl.BlockSpec((128, 3 * R_WIDTH), const),
            vec, vec, vec, vec, vec, vec, vec,
        ],
        out_spec=pl.BlockSpec((tb, R_WIDTH), lambda b, i: (b * nblk + i, 0)),
        out_shape=jax.ShapeDtypeStruct((batch * seq, R_WIDTH), BF16),
        scratch=[
            pltpu.VMEM((8, R_PROJ), F32),
            pltpu.VMEM((R_WIDTH, R_WIDTH), F32),
            pltpu.VMEM((tb, R_WIDTH), F32),
        ])


def _recurrent_mixers(mlstm_args, rwkv_args, batch, seq, tb=512):
    mp = _mlstm_parts(*mlstm_args, batch, seq, tb)
    rp = _rwkv_parts(*rwkv_args, batch, seq, tb)
    n_m_in, n_r_in = len(mp["args"]), len(rp["args"])
    n_m_scr = len(mp["scratch"])

    def kern(*refs):
        m_in = refs[:n_m_in]
        r_in = refs[n_m_in:n_m_in + n_r_in]
        om_ref, or_ref = refs[n_m_in + n_r_in:n_m_in + n_r_in + 2]
        scr = refs[n_m_in + n_r_in + 2:]
        rwkv = _rwkv_kernel(*r_in, or_ref, *scr[n_m_scr:], tb=tb)
        mlstm = _mlstm_kernel(*m_in, om_ref, *scr[:n_m_scr], tb=tb)
        live = [rwkv, mlstm]
        while live:
            for gen, turns in ((rwkv, RWKV_TURNS_PER_MLSTM_TURN), (mlstm, 1)):
                for _ in range(turns):
                    if gen in live and next(gen, "done") == "done":
                        live.remove(gen)

    return pl.pallas_call(
        kern,
        grid=(batch, seq // tb),
        in_specs=mp["in_specs"] + rp["in_specs"],
        out_specs=[mp["out_spec"], rp["out_spec"]],
        out_shape=[mp["out_shape"], rp["out_shape"]],
        scratch_shapes=mp["scratch"] + rp["scratch"],
        compiler_params=pltpu.CompilerParams(
            dimension_semantics=("parallel", "arbitrary"), vmem_limit_bytes=VMEM_LIMIT),
        name="recurrent_mixers",
    )(*mp["args"], *rp["args"])


def _attn_kernel(lq1_ref, lk1_ref, lq2_ref, lk2_ref, g_ref, q_ref, k_ref, v_ref, o_ref,
                 vt_ref, sa_ref, sb_ref, p_ref, m_ref, acc_ref, *, tq, lam_init):
    qi = pl.program_id(2)
    dv = 2 * HEAD_DIM

    @pl.when(qi == 0)
    def _():
        for jj in range(vt_ref.shape[0]):
            vt_ref[jj, :dv, :] = v_ref[jj * tq:(jj + 1) * tq, :].astype(F32).T.astype(BF16)
            vt_ref[jj, dv:, :] = jnp.ones((ATTN_SUM_ROWS, tq), BF16)

    lam = (jnp.exp(jnp.sum(lq1_ref[...] * lk1_ref[...], axis=-1, keepdims=True))
           - jnp.exp(jnp.sum(lq2_ref[...] * lk2_ref[...], axis=-1, keepdims=True)) + lam_init)
    q = q_ref[...]
    lane = _iota2(q.shape, 1)
    zero = jnp.zeros_like(q)
    q_halves = (jnp.where(lane < HEAD_DIM, q, zero), jnp.where(lane >= HEAD_DIM, q, zero))
    strips = [slice(r * ATTN_STRIP, (r + 1) * ATTN_STRIP) for r in range(tq // ATTN_STRIP)]
    groups = ATTN_STRIP // 8

    def put_scores(j, dst_ref):
        start = pl.multiple_of(j * tq, tq)
        kb = k_ref[pl.ds(start, tq), :]
        for i in range(2):
            dst_ref[i] = _dot_nt(kb, q_halves[i])

    def strip_scores(src_ref, i, rows, masked):
        s = src_ref[i, rows, :]
        if masked:
            keep = _iota2((ATTN_STRIP, tq), 0) + rows.start <= _iota2((ATTN_STRIP, tq), 1)
            s = jnp.where(keep, s, -jnp.inf)
        return s.reshape(groups, 8, tq)

    def update(j, src_ref, masked):
        vtb = vt_ref[j]
        for i in range(2):
            m = m_ref[i]
            mx8 = None
            for rows in strips:
                t = jnp.max(strip_scores(src_ref, i, rows, masked), axis=0)
                mx8 = t if mx8 is None else jnp.maximum(mx8, t)
            m_new = jnp.maximum(m, jnp.max(mx8, axis=0, keepdims=True))
            for rows in strips:
                pr = jnp.exp(strip_scores(src_ref, i, rows, masked) - m_new)
                p_ref[i, rows, :] = pr.reshape(ATTN_STRIP, tq).astype(BF16)
            m_ref[i] = m_new
            acc_ref[i] = jnp.exp(m - m_new) * acc_ref[i] + _dot(vtb, p_ref[i])

    m_ref[...] = jnp.full(m_ref.shape, -1e30, F32)
    acc_ref[...] = jnp.zeros_like(acc_ref)

    def body(t, carry):
        j = 2 * t
        put_scores(j + 1, sb_ref)
        update(j, sa_ref, False)
        put_scores(j + 2, sa_ref)
        update(j + 1, sb_ref, False)
        return carry

    put_scores(0, sa_ref)
    lax.fori_loop(0, qi // 2, body, 0)

    @pl.when(qi % 2 == 1)
    def _():
        put_scores(qi, sb_ref)
        update(qi - 1, sa_ref, False)
        update(qi, sb_ref, True)

    @pl.when(qi % 2 == 0)
    def _():
        update(qi, sa_ref, True)

    o = (acc_ref[0, :dv, :] / acc_ref[0, dv:dv + 1, :]
         - lam * (acc_ref[1, :dv, :] / acc_ref[1, dv:dv + 1, :])).T
    o = o * lax.rsqrt(jnp.mean(o * o, axis=-1, keepdims=True) + A_NORM_EPS) * g_ref[...]
    o_ref[...] = (o * (1.0 - lam_init)).astype(o_ref.dtype)


def _diff_attn(qkv, lq1, lk1, lq2, lk2, norm_g, lam_init, batch, seq, tq=512):
    nq = seq // tq
    kern = functools.partial(_attn_kernel, tq=tq, lam_init=lam_init)
    const = lambda b, h, i: (0, 0)
    lvec = pl.BlockSpec((1, HEAD_DIM), const)
    return pl.pallas_call(
        kern,
        grid=(batch, N_HEADS, nq),
        in_specs=[
            lvec, lvec, lvec, lvec,
            pl.BlockSpec((1, 128), lambda b, h, i: (0, h)),
            pl.BlockSpec((tq, 128), lambda b, h, i: (b * nq + i, h)),
            pl.BlockSpec((seq, 128), lambda b, h, i: (b, N_HEADS + h)),
            pl.BlockSpec((seq, 128), lambda b, h, i: (b, 2 * N_HEADS + h)),
        ],
        out_specs=pl.BlockSpec((tq, 128), lambda b, h, i: (b * nq + i, h)),
        out_shape=jax.ShapeDtypeStruct((batch * seq, A_WIDTH), BF16),
        scratch_shapes=[
            pltpu.VMEM((nq, 2 * HEAD_DIM + ATTN_SUM_ROWS, tq), BF16),
            pltpu.VMEM((2, tq, tq), F32),
            pltpu.VMEM((2, tq, tq), F32),
            pltpu.VMEM((2, tq, tq), BF16),
            pltpu.VMEM((2, 1, tq), F32),
            pltpu.VMEM((2, 2 * HEAD_DIM + ATTN_SUM_ROWS, tq), F32),
        ],
        compiler_params=pltpu.CompilerParams(
            dimension_semantics=("parallel", "parallel", "arbitrary"), vmem_limit_bytes=VMEM_LIMIT),
        name="diff_attn",
    )(lq1[None, :], lk1[None, :], lq2[None, :], lk2[None, :], norm_g[None, :], qkv, qkv, qkv)


def _out_ffn_kernel(x_ref, ym_ref, yr_ref, ya_ref, wo_ref, g2_ref, wu_ref, wd_ref, gf_ref, o_ref,
                    h_ref, acc_ref, *, final_norm):
    j = pl.program_id(1)

    @pl.when(j == 0)
    def _():
        xn = (x_ref[...]
              + _dot(ym_ref[...], wo_ref[0:M_WIDTH, :])
              + _dot(yr_ref[...], wo_ref[M_WIDTH:M_WIDTH + R_WIDTH, :])
              + _dot(ya_ref[...], wo_ref[M_WIDTH + R_WIDTH:, :]))
        acc_ref[...] = xn
        hn = xn * lax.rsqrt(jnp.mean(xn * xn, axis=-1, keepdims=True) + NORM_EPS) * g2_ref[...]
        h_ref[...] = hn.astype(BF16)

    u = jnp.maximum(_dot(h_ref[...], wu_ref[...]), 0.0)
    acc_ref[...] += _dot((u * u).astype(BF16), wd_ref[...])

    @pl.when(j == pl.num_programs(1) - 1)
    def _():
        y = acc_ref[...]
        if final_norm:
            y = y * lax.rsqrt(jnp.mean(y * y, axis=-1, keepdims=True) + NORM_EPS) * gf_ref[...]
        o_ref[...] = y


def _out_ffn(x, ym, yr, ya, w_out, g2, w_up, w_down, g_final, final_norm, tm=1024, tf=512):
    t, d = x.shape
    f = w_up.shape[1]
    kern = functools.partial(_out_ffn_kernel, final_norm=final_norm)
    rows = lambda i, j: (i, 0)
    const = lambda i, j: (0, 0)
    return pl.pallas_call(
        kern,
        grid=(t // tm, f // tf),
        in_specs=[
            pl.BlockSpec((tm, d), rows),
            pl.BlockSpec((tm, M_WIDTH), rows),
            pl.BlockSpec((tm, R_WIDTH), rows),
            pl.BlockSpec((tm, A_WIDTH), rows),
            pl.BlockSpec((d, d), const),
            pl.BlockSpec((1, d), const),
            pl.BlockSpec((d, tf), lambda i, j: (0, j)),
            pl.BlockSpec((tf, d), lambda i, j: (j, 0)),
            pl.BlockSpec((1, d), const),
        ],
        out_specs=pl.BlockSpec((tm, d), rows),
        out_shape=jax.ShapeDtypeStruct((t, d), F32),
        scratch_shapes=[pltpu.VMEM((tm, d), BF16), pltpu.VMEM((tm, d), F32)],
        compiler_params=pltpu.CompilerParams(
            dimension_semantics=("parallel", "arbitrary"), vmem_limit_bytes=VMEM_LIMIT),
        name="out_ffn",
    )(x, ym, yr, ya, w_out, g2, w_up, w_down, g_final)


def _split_in_proj(w):
    m_main = w[:, 0:4 * M_WIDTH]
    m_gate = w[:, 4 * M_WIDTH:4 * M_WIDTH + 2 * N_HEADS]
    r0 = 4 * M_WIDTH + 2 * N_HEADS
    r_all = w[:, r0:r0 + R_PROJ]
    a0 = r0 + R_PROJ
    a_q = w[:, a0:a0 + A_WIDTH] * (HEAD_DIM ** -0.5)
    a_kv = w[:, a0 + A_WIDTH:a0 + A_PROJ]
    pad = jnp.zeros((w.shape[0], 128 - 2 * N_HEADS), w.dtype)
    w_main = jnp.concatenate([r_all, m_gate, pad, m_main], axis=1).astype(BF16)
    w_attn = jnp.concatenate([a_q, a_kv], axis=1).astype(BF16)
    return w_main, w_attn


def kernel(x, norm1_g, w_in, m_conv_w, m_conv_b, m_b_i, m_b_f, m_norm_g, r_mu, r_w0, r_w_up, r_a0, r_a_up, r_g_up, r_k_k, r_k_a, r_r_k, r_ln_g, r_ln_b, a_lq1, a_lk1, a_lq2, a_lk2, a_norm_g, w_out, norm2_g, w_ff_up, w_ff_down, final_g):
    batch, seq, d = x.shape
    depth = w_in.shape[0]
    xt = x.reshape(batch * seq, d)
    for l in range(depth):
        w_main, w_attn = _split_in_proj(w_in[l])
        g1 = norm1_g[l][None, :]
        proj, qkv = _in_proj(xt, g1, w_main, w_attn)
        gates_r = (proj[:, R_PROJ:R_PROJ + 2 * N_HEADS]
                   .reshape(batch * seq // CHUNK, CHUNK, 2 * N_HEADS).transpose(0, 2, 1)
                   .reshape(batch * seq // CHUNK, 2 * N_HEADS * CHUNK))
        y_m, y_r = _recurrent_mixers(
            (proj, gates_r, m_conv_w[l], m_conv_b[l], m_b_i[l], m_b_f[l], m_norm_g[l]),
            (proj, r_mu[l], r_w0[l], r_w_up[l], r_a0[l], r_a_up[l], r_g_up[l],
             r_k_k[l], r_k_a[l], r_r_k[l], r_ln_g[l], r_ln_b[l]),
            batch, seq)
        lam_init = 0.8 - 0.6 * math.exp(-0.3 * l)
        y_a = _diff_attn(qkv, a_lq1[l], a_lk1[l], a_lq2[l], a_lk2[l], a_norm_g[l], lam_init, batch, seq)
        xt = _out_ffn(xt, y_m, y_r, y_a, w_out[l].astype(BF16), norm2_g[l][None, :],
                      w_ff_up[l].astype(BF16), w_ff_down[l].astype(BF16), final_g[None, :],
                      final_norm=(l == depth - 1))
    return xt.reshape(batch, seq, d)
```

```python
import functools
import math

import jax
import jax.numpy as jnp
from jax import lax
from jax.experimental import pallas as pl
from jax.experimental.pallas import tpu as pltpu

F32 = jnp.float32
BF16 = jnp.bfloat16
HI = lax.Precision.HIGHEST

D_MODEL = 1024
HEAD_DIM = 64
N_HEADS = 4
M_WIDTH = 256
R_WIDTH = 256
R_PROJ = 896
A_WIDTH = 512
A_PROJ = 1536
D_FF = 4096
CHUNK = 64
ATTN_STRIP = 64
RWKV_TURNS_PER_MLSTM_TURN = 6
ATTN_HEADS_PER_STEP = 4
ATTN_SUM_ROWS = 16
NORM_EPS = 1e-6
M_NORM_EPS = 1e-6
R_LN_EPS = 64e-5
A_NORM_EPS = 1e-5
MAIN_PROJ = 2048
GATE_COL_BLOCK = 7
VMEM_LIMIT = 52 * 1024 * 1024


def _dot(a, b, prec=None):
    return jnp.dot(a, b, preferred_element_type=F32, precision=prec)


def _dot_nt(a, b, prec=None):
    return lax.dot_general(a, b, (((1,), (1,)), ((), ())), preferred_element_type=F32, precision=prec)


def _dot_tn(a, b, prec=None):
    return lax.dot_general(a, b, (((0,), (0,)), ((), ())), preferred_element_type=F32, precision=prec)


def _split3(x):
    hi = x.astype(BF16)
    rem = x - hi.astype(F32)
    mid = rem.astype(BF16)
    lo = (rem - mid.astype(F32)).astype(BF16)
    return hi, mid, lo


def _dot_f32_lhs(a, b_bf16):
    hi, mid, lo = _split3(a)
    return _dot(hi, b_bf16) + _dot(mid, b_bf16) + _dot(lo, b_bf16)


def _dot_f32_rhs(a_bf16, b):
    hi, mid, lo = _split3(b)
    return _dot(a_bf16, hi) + _dot(a_bf16, mid) + _dot(a_bf16, lo)


def _dot_x3(a, b):
    ah = a.astype(BF16)
    bh = b.astype(BF16)
    al = (a - ah.astype(F32)).astype(BF16)
    bl = (b - bh.astype(F32)).astype(BF16)
    return _dot(ah, bh) + _dot(ah, bl) + _dot(al, bh)


def _sigmoid(x):
    return 1.0 / (1.0 + jnp.exp(-x))


def _log_sigmoid(x):
    return jnp.minimum(x, 0.0) - jnp.log(1.0 + jnp.exp(-jnp.abs(x)))


def _iota2(shape, dim):
    return lax.broadcasted_iota(jnp.int32, shape, dim)


def _in_proj_kernel(x_ref, g_ref, wm_ref, wa_ref, om_ref, oa_ref):
    x = x_ref[...]
    h = (x * lax.rsqrt(jnp.mean(x * x, axis=-1, keepdims=True) + NORM_EPS) * g_ref[...]).astype(BF16)
    om_ref[...] = _dot(h, wm_ref[...])
    oa_ref[...] = _dot(h, wa_ref[...]).astype(oa_ref.dtype)


def _in_proj(x, g, w_main, w_attn, tm=512):
    t, d = x.shape
    rows = lambda i: (i, 0)
    const = lambda i: (0, 0)
    return pl.pallas_call(
        _in_proj_kernel,
        grid=(t // tm,),
        in_specs=[
            pl.BlockSpec((tm, d), rows),
            pl.BlockSpec((1, d), const),
            pl.BlockSpec((d, MAIN_PROJ), const),
            pl.BlockSpec((d, A_PROJ), const),
        ],
        out_specs=[pl.BlockSpec((tm, MAIN_PROJ), rows), pl.BlockSpec((tm, A_PROJ), rows)],
        out_shape=[jax.ShapeDtypeStruct((t, MAIN_PROJ), F32), jax.ShapeDtypeStruct((t, A_PROJ), BF16)],
        compiler_params=pltpu.CompilerParams(
            dimension_semantics=("parallel",), vmem_limit_bytes=VMEM_LIMIT),
        name="in_proj",
    )(x, g, w_main, w_attn)


def _mlstm_kernel(x_ref, gc_ref, gr_ref, cw_ref, cb_ref, bc_ref, br_ref, ng_ref, o_ref,
                  carry_ref, c_ref, n_ref, m_ref, *, tb):
    @pl.when(pl.program_id(1) == 0)
    def _():
        carry_ref[...] = jnp.zeros_like(carry_ref)
        c_ref[...] = jnp.zeros_like(c_ref)
        n_ref[...] = jnp.zeros_like(n_ref)
        m_ref[...] = jnp.zeros_like(m_ref)

    x = x_ref[...]
    qk_pre = x[:, :2 * M_WIDTH]
    ext = jnp.concatenate([carry_ref[...], qk_pre], axis=0)
    carry_ref[...] = qk_pre[tb - 8:, :]
    cw = cw_ref[...]
    acc = cb_ref[...] + cw[3:4, :] * qk_pre
    for j in (1, 2, 3):
        acc = acc + cw[3 - j:4 - j, :] * pltpu.roll(ext, j, axis=0)[8:, :]
    qk = acc * _sigmoid(acc)
    q = qk[:, :M_WIDTH]
    k = qk[:, M_WIDTH:] * (HEAD_DIM ** -0.5)
    v = x[:, 2 * M_WIDTH:3 * M_WIDTH]
    o_gate = _sigmoid(x[:, 3 * M_WIDTH:])

    n = N_HEADS * CHUNK
    row = _iota2((n, n), 0)
    col = _iota2((n, n), 1)
    same_head = (row >> 6) == (col >> 6)
    causal = same_head & ((col & (CHUNK - 1)) <= (row & (CHUNK - 1)))
    eye = row == col
    triu_bd = (same_head & ((row & (CHUNK - 1)) <= (col & (CHUNK - 1)))).astype(BF16)
    tril64 = (_iota2((CHUNK, CHUNK), 1) <= _iota2((CHUNK, CHUNK), 0)).astype(BF16)
    lane_g = _iota2((n, 128), 1)
    head_g = _iota2((n, 128), 0) >> 6
    pick_i = lane_g == head_g
    pick_f = lane_g == head_g + N_HEADS
    ng = ng_ref[...]

    def tile4(z):
        return jnp.concatenate([z, z, z, z], axis=0)

    def col_of(z, pick):
        zz = tile4(z) if z.shape[0] == CHUNK else z
        return jnp.sum(jnp.where(pick, zz, 0.0), axis=-1, keepdims=True)

    def to_row(z_col):
        return jnp.sum(jnp.where(eye, z_col, 0.0), axis=0, keepdims=True)

    gc = gc_ref[...] + bc_ref[...]
    gc = jnp.where(_iota2(gc.shape, 1) < N_HEADS, gc, _log_sigmoid(gc))
    gr = gr_ref[...] + br_ref[...]
    ig_rows = gr[:, :n]
    b_rows = _dot_f32_lhs(_log_sigmoid(gr[:, n:]), triu_bd)

    m_col = m_ref[...]
    pre = []
    for c in range(tb // CHUNK):
        rs = slice(c * CHUNK, (c + 1) * CHUNK)
        gcc = gc[rs, :]
        bc = _dot_f32_rhs(tril64, gcc)
        b_col = col_of(bc, pick_f)
        ig_col = col_of(gcc, pick_i)
        bl_col = col_of(bc[CHUNK - 1:CHUNK, :], pick_f)
        g_end = bl_col - b_col + ig_col
        seg_max = jnp.concatenate(
            [jnp.broadcast_to(jnp.max(g_end[h * CHUNK:(h + 1) * CHUNK, :], axis=0, keepdims=True), (CHUNK, 1))
             for h in range(N_HEADS)], axis=0)
        m_new = jnp.maximum(bl_col + m_col, seg_max)
        pre.append(dict(rs=rs, b_col=b_col, m_col=m_col, wk=jnp.exp(g_end - m_new),
                        cs_col=jnp.exp(bl_col + m_col - m_new),
                        b_row=b_rows[c:c + 1, :], ig_row=ig_rows[c:c + 1, :]))
        m_col = m_new
    m_ref[...] = m_col
    yield

    for ch in pre:
        rs = ch["rs"]
        dmat = jnp.where(causal, ch["b_col"] - ch["b_row"] + ch["ig_row"], -jnp.inf)
        inter = ch["b_col"] + ch["m_col"]
        mt = jnp.maximum(inter, jnp.max(dmat, axis=-1, keepdims=True))
        dw = jnp.exp(dmat - mt)
        iw = jnp.exp(inter - mt)
        q_f = jnp.where(same_head, tile4(q[rs, :]), 0.0)
        k_f = jnp.where(same_head, tile4(k[rs, :]), 0.0)
        q_bd = q_f.astype(BF16)
        v_bd = jnp.where(same_head, tile4(v[rs, :]), 0.0).astype(BF16)
        sqk = _dot_nt(q_bd, tile4(k[rs, :].astype(BF16))) * dw
        cmat = c_ref[...]
        nvec = n_ref[...]
        num = iw * _dot(q_bd, cmat.astype(BF16)) + _dot(sqk.astype(BF16), v_bd)
        den = (iw * jnp.sum(q_f * nvec, axis=-1, keepdims=True)
               + jnp.sum(sqk, axis=-1, keepdims=True))
        hh = num / jnp.maximum(jnp.abs(den), jnp.exp(-mt))
        kw = k_f * ch["wk"]
        c_ref[...] = ch["cs_col"] * cmat + _dot_tn(kw.astype(BF16), v_bd)
        n_ref[...] = to_row(ch["cs_col"]) * nvec + jnp.sum(kw, axis=0, keepdims=True)
        hn = hh * lax.rsqrt(jnp.sum(hh * hh, axis=-1, keepdims=True) * (1.0 / HEAD_DIM) + M_NORM_EPS)
        hc = (hn[0:CHUNK, :] + hn[CHUNK:2 * CHUNK, :] + hn[2 * CHUNK:3 * CHUNK, :] + hn[3 * CHUNK:, :])
        o_ref[rs, :] = (hc * ng * o_gate[rs, :]).astype(o_ref.dtype)
        yield


def _mlstm_parts(proj, gates_r, conv_w, conv_b, b_i, b_f, norm_g, batch, seq, tb):
    nblk = seq // tb
    bias = jnp.concatenate([b_i, b_f]).astype(F32)
    bias_lanes = jnp.pad(bias, (0, 128 - 2 * N_HEADS))[None, :]
    bias_r = jnp.repeat(bias, CHUNK)[None, :]
    const = lambda b, i: (0, 0)
    return dict(
        args=(proj, proj, gates_r, conv_w, conv_b[None, :], bias_lanes, bias_r, norm_g[None, :]),
        in_specs=[
            pl.BlockSpec((tb, 4 * M_WIDTH), lambda b, i: (b * nblk + i, 1)),
            pl.BlockSpec((tb, 128), lambda b, i: (b * nblk + i, GATE_COL_BLOCK)),
            pl.BlockSpec((tb // CHUNK, 2 * N_HEADS * CHUNK), lambda b, i: (b * nblk + i, 0)),
            pl.BlockSpec((4, 2 * M_WIDTH), const),
            pl.BlockSpec((1, 2 * M_WIDTH), const),
            pl.BlockSpec((1, 128), const),
            pl.BlockSpec((1, 2 * N_HEADS * CHUNK), const),
            pl.BlockSpec((1, M_WIDTH), const),
        ],
        out_spec=pl.BlockSpec((tb, M_WIDTH), lambda b, i: (b * nblk + i, 0)),
        out_shape=jax.ShapeDtypeStruct((batch * seq, M_WIDTH), BF16),
        scratch=[
            pltpu.VMEM((8, 2 * M_WIDTH), F32),
            pltpu.VMEM((M_WIDTH, M_WIDTH), F32),
            pltpu.VMEM((1, M_WIDTH), F32),
            pltpu.VMEM((N_HEADS * CHUNK, 1), F32),
        ])


def _rwkv_kernel(p_ref, mu_ref, wl_ref, w0_ref, a0_ref, kk_ref, ka_ref, rk_ref, lng_ref, lnb_ref,
                 o_ref, carry_ref, s_ref, y_ref, *, tb):
    @pl.when(pl.program_id(1) == 0)
    def _():
        carry_ref[...] = jnp.zeros_like(carry_ref)
        s_ref[...] = jnp.zeros_like(s_ref)

    p = p_ref[...]
    ext = jnp.concatenate([carry_ref[...], p], axis=0)
    carry_ref[...] = p[tb - 8:, :]
    prev = pltpu.roll(ext, 1, axis=0)[8:, :]
    pm = p + (prev - p) * mu_ref[...]
    r = pm[:, :R_WIDTH]
    k = pm[:, R_WIDTH:2 * R_WIDTH]
    v = pm[:, 2 * R_WIDTH:3 * R_WIDTH]
    lo = pm[:, 3 * R_WIDTH:]
    lane = _iota2(lo.shape, 1)
    act = jnp.where(lane < 32, jnp.tanh(lo), jnp.where(lane < 64, lo, _sigmoid(lo)))
    lora = _dot_x3(act, wl_ref[...])
    w_log = _log_sigmoid(w0_ref[...] + lora[:, :R_WIDTH]) - 0.5
    lw = -jnp.exp(w_log)
    a = _sigmoid(a0_ref[...] + lora[:, R_WIDTH:2 * R_WIDTH])
    gate = lora[:, 2 * R_WIDTH:]

    n = N_HEADS * CHUNK
    row = _iota2((n, n), 0)
    col = _iota2((n, n), 1)
    same_head = (row >> 6) == (col >> 6)
    t_row = row & (CHUNK - 1)
    t_col = col & (CHUNK - 1)
    incl = same_head & (t_col <= t_row)
    strict = same_head & (t_col < t_row)
    eye = row == col
    eye_f = eye.astype(F32)
    head_ones = same_head.astype(BF16)
    n_levels = int(math.log2(CHUNK))
    level_masks = [
        same_head & ((t_row >> (l + 1)) == (t_col >> (l + 1)))
        & (((t_row >> l) & 1) == 1) & (((t_col >> l) & 1) == 0)
        for l in range(n_levels)
    ]
    tril64 = (_iota2((CHUNK, CHUNK), 1) <= _iota2((CHUNK, CHUNK), 0)).astype(BF16)

    kk = k * kk_ref[...]
    kk = kk / jnp.maximum(jnp.sqrt(_dot_f32_lhs(kk * kk, head_ones)), 1e-12)
    k2 = k * (1.0 + (a - 1.0) * ka_ref[...])
    bvec = kk * a

    def tile4(z):
        return jnp.concatenate([z, z, z, z], axis=0)

    def block_diag(z):
        return jnp.where(same_head, tile4(z), 0.0).astype(BF16)

    chunks = []
    for c in range(tb // CHUNK):
        rs = slice(c * CHUNK, (c + 1) * CHUNK)
        lwc = lw[rs, :]
        g = _dot_f32_rhs(tril64, lwc)
        g_last = g[CHUNK - 1:CHUNK, :]
        e_pos = jnp.exp(g)
        e_neg = jnp.exp(-g)
        e_end = jnp.exp(g_last - g)
        e_last = jnp.exp(g_last)
        ar = jnp.concatenate([block_diag(-kk[rs, :] * jnp.exp(g - lwc)),
                              block_diag(r[rs, :] * e_pos)], axis=0)
        bk = jnp.concatenate([tile4((bvec[rs, :] * e_neg).astype(BF16)),
                              tile4((k2[rs, :] * e_neg).astype(BF16))], axis=0)
        bkh = jnp.concatenate([tile4((bvec[rs, :] * e_end).astype(BF16)),
                               tile4((k2[rs, :] * e_end).astype(BF16))], axis=0)
        v_bd = block_diag(v[rs, :])
        mm = _dot_nt(ar, bk)
        ab = jnp.where(strict, mm[:n, :n], 0.0)
        ak = jnp.where(strict, mm[:n, n:], 0.0).astype(BF16)
        rbk = jnp.concatenate([jnp.where(incl, mm[n:, :n], 0.0).astype(BF16),
                               jnp.where(incl, mm[n:, n:], 0.0).astype(BF16)], axis=1)
        e_last_col = jnp.sum(jnp.where(eye, e_last, 0.0), axis=-1, keepdims=True)
        chunks.append(dict(rs=rs, ar=ar, bkh=bkh, v_bd=v_bd, ab=ab, rbk=rbk, akv=_dot(ak, v_bd),
                           e_last_col=e_last_col, tinv=eye_f + jnp.where(level_masks[0], ab, 0.0)))
        yield
    for l in range(1, n_levels):
        for ch in chunks:
            t_bf = ch["tinv"].astype(BF16)
            join = jnp.where(level_masks[l], ch["ab"], 0.0).astype(BF16)
            ch["tinv"] = ch["tinv"] + _dot(t_bf, _dot(join, t_bf).astype(BF16))
            yield
    for ch in chunks:
        h0 = s_ref[...]
        ah = _dot(ch["ar"], h0.astype(BF16))
        u = _dot(ch["tinv"].astype(BF16), (ah[:n, :] + ch["akv"]).astype(BF16))
        uv = jnp.concatenate([u.astype(BF16), ch["v_bd"]], axis=0)
        o = ah[n:, :] + _dot(ch["rbk"], uv)
        y_ref[ch["rs"], :] = (o[0:CHUNK, :] + o[CHUNK:2 * CHUNK, :]
                              + o[2 * CHUNK:3 * CHUNK, :] + o[3 * CHUNK:, :])
        s_ref[...] = h0 * ch["e_last_col"] + jnp.where(same_head, _dot_tn(ch["bkh"], uv), 0.0)
        yield

    y = y_ref[...]
    inv_d = 1.0 / HEAD_DIM
    mean = _dot_f32_lhs(y, head_ones) * inv_d
    yc = y - mean
    var = _dot_f32_lhs(yc * yc, head_ones) * inv_d
    yn = yc * lax.rsqrt(var + R_LN_EPS) * lng_ref[...] + lnb_ref[...]
    bonus = _dot_f32_lhs(r * k2 * rk_ref[...], head_ones)
    o_ref[...] = ((yn + bonus * v) * gate).astype(o_ref.dtype)


def _rwkv_parts(proj, mu, w0, w_up, a0, a_up, g_up, k_k, k_a, r_k, ln_g, ln_b, batch, seq, tb):
    nblk = seq // tb
    w_lora = jnp.zeros((128, 3 * R_WIDTH), F32)
    w_lora = w_lora.at[0:32, 0:R_WIDTH].set(w_up)
    w_lora = w_lora.at[32:64, R_WIDTH:2 * R_WIDTH].set(a_up)
    w_lora = w_lora.at[64:128, 2 * R_WIDTH:].set(g_up)
    const = lambda b, i: (0, 0)
    vec = pl.BlockSpec((1, R_WIDTH), const)
    return dict(
        args=(proj, mu[None, :], w_lora, w0[None, :], a0[None, :], k_k[None, :], k_a[None, :],
              r_k.reshape(1, R_WIDTH), ln_g[None, :], ln_b[None, :]),
        in_specs=[
            pl.BlockSpec((tb, R_PROJ), lambda b, i: (b * nblk + i, 0)),
            pl.BlockSpec((1, R_PROJ), const),
            pl.BlockSpec((128, 3 * R_WIDTH), const),
            vec, vec, vec, vec, vec, vec, vec,
        ],
        out_spec=pl.BlockSpec((tb, R_WIDTH), lambda b, i: (b * nblk + i, 0)),
        out_shape=jax.ShapeDtypeStruct((batch * seq, R_WIDTH), BF16),
        scratch=[
            pltpu.VMEM((8, R_PROJ), F32),
            pltpu.VMEM((R_WIDTH, R_WIDTH), F32),
            pltpu.VMEM((tb, R_WIDTH), F32),
        ])


def _recurrent_mixers(mlstm_args, rwkv_args, batch, seq, tb=512):
    mp = _mlstm_parts(*mlstm_args, batch, seq, tb)
    rp = _rwkv_parts(*rwkv_args, batch, seq, tb)
    n_m_in, n_r_in = len(mp["args"]), len(rp["args"])
    n_m_scr = len(mp["scratch"])

    def kern(*refs):
        m_in = refs[:n_m_in]
        r_in = refs[n_m_in:n_m_in + n_r_in]
        om_ref, or_ref = refs[n_m_in + n_r_in:n_m_in + n_r_in + 2]
        scr = refs[n_m_in + n_r_in + 2:]
        rwkv = _rwkv_kernel(*r_in, or_ref, *scr[n_m_scr:], tb=tb)
        mlstm = _mlstm_kernel(*m_in, om_ref, *scr[:n_m_scr], tb=tb)
        live = [rwkv, mlstm]
        while live:
            for gen, turns in ((rwkv, RWKV_TURNS_PER_MLSTM_TURN), (mlstm, 1)):
                for _ in range(turns):
                    if gen in live and next(gen, "done") == "done":
                        live.remove(gen)

    return pl.pallas_call(
        kern,
        grid=(batch, seq // tb),
        in_specs=mp["in_specs"] + rp["in_specs"],
        out_specs=[mp["out_spec"], rp["out_spec"]],
        out_shape=[mp["out_shape"], rp["out_shape"]],
        scratch_shapes=mp["scratch"] + rp["scratch"],
        compiler_params=pltpu.CompilerParams(
            dimension_semantics=("parallel", "arbitrary"), vmem_limit_bytes=VMEM_LIMIT),
        name="recurrent_mixers",
    )(*mp["args"], *rp["args"])


def _attn_kernel(lq1_ref, lk1_ref, lq2_ref, lk2_ref, g_ref, q_ref, k_ref, v_ref, o_ref,
                 vt_ref, sa_ref, sb_ref, p_ref, m_ref, acc_ref, *, tq, lam_init):
    qi = pl.program_id(2)
    dv = 2 * HEAD_DIM
    heads = range(ATTN_HEADS_PER_STEP)
    streams = range(2 * ATTN_HEADS_PER_STEP)

    @pl.when(qi == 0)
    def _():
        for hh in heads:
            for jj in range(vt_ref.shape[1]):
                v_blk = v_ref[jj * tq:(jj + 1) * tq, hh * dv:(hh + 1) * dv]
                vt_ref[hh, jj, :dv, :] = v_blk.astype(F32).T.astype(BF16)
                vt_ref[hh, jj, dv:, :] = jnp.ones((ATTN_SUM_ROWS, tq), BF16)

    lam = (jnp.exp(jnp.sum(lq1_ref[...] * lk1_ref[...], axis=-1, keepdims=True))
           - jnp.exp(jnp.sum(lq2_ref[...] * lk2_ref[...], axis=-1, keepdims=True)) + lam_init)
    lane = _iota2((tq, dv), 1)
    q_halves = []
    for hh in heads:
        q = q_ref[:, hh * dv:(hh + 1) * dv]
        zero = jnp.zeros_like(q)
        q_halves += [jnp.where(lane < HEAD_DIM, q, zero), jnp.where(lane >= HEAD_DIM, q, zero)]
    strips = [slice(r * ATTN_STRIP, (r + 1) * ATTN_STRIP) for r in range(tq // ATTN_STRIP)]
    groups = ATTN_STRIP // 8

    def put_scores(j, dst_ref):
        start = pl.multiple_of(j * tq, tq)
        for hh in heads:
            kb = k_ref[pl.ds(start, tq), hh * dv:(hh + 1) * dv]
            for u in (2 * hh, 2 * hh + 1):
                dst_ref[u] = _dot_nt(kb, q_halves[u])

    def strip_scores(src_ref, i, rows, masked):
        s = src_ref[i, rows, :]
        if masked:
            keep = _iota2((ATTN_STRIP, tq), 0) + rows.start <= _iota2((ATTN_STRIP, tq), 1)
            s = jnp.where(keep, s, -jnp.inf)
        return s.reshape(groups, 8, tq)

    def update(j, src_ref, masked):
        for i in streams:
            m = m_ref[i]
            mx8 = None
            for rows in strips:
                t = jnp.max(strip_scores(src_ref, i, rows, masked), axis=0)
                mx8 = t if mx8 is None else jnp.maximum(mx8, t)
            m_new = jnp.maximum(m, jnp.max(mx8, axis=0, keepdims=True))
            for rows in strips:
                pr = jnp.exp(strip_scores(src_ref, i, rows, masked) - m_new)
                p_ref[i, rows, :] = pr.reshape(ATTN_STRIP, tq).astype(BF16)
            m_ref[i] = m_new
            acc_ref[i] = jnp.exp(m - m_new) * acc_ref[i] + _dot(vt_ref[i // 2, j], p_ref[i])

    m_ref[...] = jnp.full(m_ref.shape, -1e30, F32)
    acc_ref[...] = jnp.zeros_like(acc_ref)

    def body(t, carry):
        j = 2 * t
        put_scores(j + 1, sb_ref)
        update(j, sa_ref, False)
        put_scores(j + 2, sa_ref)
        update(j + 1, sb_ref, False)
        return carry

    put_scores(0, sa_ref)
    lax.fori_loop(0, qi // 2, body, 0)

    @pl.when(qi % 2 == 1)
    def _():
        put_scores(qi, sb_ref)
        update(qi - 1, sa_ref, False)
        update(qi, sb_ref, True)

    @pl.when(qi % 2 == 0)
    def _():
        update(qi, sa_ref, True)

    for hh in heads:
        a1, a2 = acc_ref[2 * hh], acc_ref[2 * hh + 1]
        o = (a1[:dv, :] / a1[dv:dv + 1, :] - lam * (a2[:dv, :] / a2[dv:dv + 1, :])).T
        o = o * lax.rsqrt(jnp.mean(o * o, axis=-1, keepdims=True) + A_NORM_EPS) * g_ref[:, hh * dv:(hh + 1) * dv]
        o_ref[:, hh * dv:(hh + 1) * dv] = (o * (1.0 - lam_init)).astype(o_ref.dtype)


def _diff_attn(qkv, lq1, lk1, lq2, lk2, norm_g, lam_init, batch, seq, tq=512):
    nq = seq // tq
    nh = ATTN_HEADS_PER_STEP
    n_groups = N_HEADS // nh
    wide = nh * 2 * HEAD_DIM
    kern = functools.partial(_attn_kernel, tq=tq, lam_init=lam_init)
    const = lambda b, g, i: (0, 0)
    lvec = pl.BlockSpec((1, HEAD_DIM), const)
    return pl.pallas_call(
        kern,
        grid=(batch, n_groups, nq),
        in_specs=[
            lvec, lvec, lvec, lvec,
            pl.BlockSpec((1, wide), lambda b, g, i: (0, g)),
            pl.BlockSpec((tq, wide), lambda b, g, i: (b * nq + i, g)),
            pl.BlockSpec((seq, wide), lambda b, g, i: (b, n_groups + g)),
            pl.BlockSpec((seq, wide), lambda b, g, i: (b, 2 * n_groups + g)),
        ],
        out_specs=pl.BlockSpec((tq, wide), lambda b, g, i: (b * nq + i, g)),
        out_shape=jax.ShapeDtypeStruct((batch * seq, A_WIDTH), BF16),
        scratch_shapes=[
            pltpu.VMEM((nh, nq, 2 * HEAD_DIM + ATTN_SUM_ROWS, tq), BF16),
            pltpu.VMEM((2 * nh, tq, tq), F32),
            pltpu.VMEM((2 * nh, tq, tq), F32),
            pltpu.VMEM((2 * nh, tq, tq), BF16),
            pltpu.VMEM((2 * nh, 1, tq), F32),
            pltpu.VMEM((2 * nh, 2 * HEAD_DIM + ATTN_SUM_ROWS, tq), F32),
        ],
        compiler_params=pltpu.CompilerParams(
            dimension_semantics=("parallel", "parallel", "arbitrary"), vmem_limit_bytes=VMEM_LIMIT),
        name="diff_attn",
    )(lq1[None, :], lk1[None, :], lq2[None, :], lk2[None, :], norm_g[None, :], qkv, qkv, qkv)


def _out_ffn_kernel(x_ref, ym_ref, yr_ref, ya_ref, wo_ref, g2_ref, wu_ref, wd_ref, gf_ref, o_ref,
                    h_ref, acc_ref, *, final_norm):
    j = pl.program_id(1)

    @pl.when(j == 0)
    def _():
        xn = (x_ref[...]
              + _dot(ym_ref[...], wo_ref[0:M_WIDTH, :])
              + _dot(yr_ref[...], wo_ref[M_WIDTH:M_WIDTH + R_WIDTH, :])
              + _dot(ya_ref[...], wo_ref[M_WIDTH + R_WIDTH:, :]))
        acc_ref[...] = xn
        hn = xn * lax.rsqrt(jnp.mean(xn * xn, axis=-1, keepdims=True) + NORM_EPS) * g2_ref[...]
        h_ref[...] = hn.astype(BF16)

    u = jnp.maximum(_dot(h_ref[...], wu_ref[...]), 0.0)
    acc_ref[...] += _dot((u * u).astype(BF16), wd_ref[...])

    @pl.when(j == pl.num_programs(1) - 1)
    def _():
        y = acc_ref[...]
        if final_norm:
            y = y * lax.rsqrt(jnp.mean(y * y, axis=-1, keepdims=True) + NORM_EPS) * gf_ref[...]
        o_ref[...] = y


def _out_ffn(x, ym, yr, ya, w_out, g2, w_up, w_down, g_final, final_norm, tm=1024, tf=1024):
    t, d = x.shape
    f = w_up.shape[1]
    kern = functools.partial(_out_ffn_kernel, final_norm=final_norm)
    rows = lambda i, j: (i, 0)
    const = lambda i, j: (0, 0)
    return pl.pallas_call(
        kern,
        grid=(t // tm, f // tf),
        in_specs=[
            pl.BlockSpec((tm, d), rows),
            pl.BlockSpec((tm, M_WIDTH), rows),
            pl.BlockSpec((tm, R_WIDTH), rows),
            pl.BlockSpec((tm, A_WIDTH), rows),
            pl.BlockSpec((d, d), const),
            pl.BlockSpec((1, d), const),
            pl.BlockSpec((d, tf), lambda i, j: (0, j)),
            pl.BlockSpec((tf, d), lambda i, j: (j, 0)),
            pl.BlockSpec((1, d), const),
        ],
        out_specs=pl.BlockSpec((tm, d), rows),
        out_shape=jax.ShapeDtypeStruct((t, d), F32),
        scratch_shapes=[pltpu.VMEM((tm, d), BF16), pltpu.VMEM((tm, d), F32)],
        compiler_params=pltpu.CompilerParams(
            dimension_semantics=("parallel", "arbitrary"), vmem_limit_bytes=VMEM_LIMIT),
        name="out_ffn",
    )(x, ym, yr, ya, w_out, g2, w_up, w_down, g_final)


def _split_in_proj(w):
    m_main = w[:, 0:4 * M_WIDTH]
    m_gate = w[:, 4 * M_WIDTH:4 * M_WIDTH + 2 * N_HEADS]
    r0 = 4 * M_WIDTH + 2 * N_HEADS
    r_all = w[:, r0:r0 + R_PROJ]
    a0 = r0 + R_PROJ
    a_q = w[:, a0:a0 + A_WIDTH] * (HEAD_DIM ** -0.5)
    a_kv = w[:, a0 + A_WIDTH:a0 + A_PROJ]
    pad = jnp.zeros((w.shape[0], 128 - 2 * N_HEADS), w.dtype)
    w_main = jnp.concatenate([r_all, m_gate, pad, m_main], axis=1).astype(BF16)
    w_attn = jnp.concatenate([a_q, a_kv], axis=1).astype(BF16)
    return w_main, w_attn


def kernel(x, norm1_g, w_in, m_conv_w, m_conv_b, m_b_i, m_b_f, m_norm_g, r_mu, r_w0, r_w_up, r_a0, r_a_up, r_g_up, r_k_k, r_k_a, r_r_k, r_ln_g, r_ln_b, a_lq1, a_lk1, a_lq2, a_lk2, a_norm_g, w_out, norm2_g, w_ff_up, w_ff_down, final_g):
    batch, seq, d = x.shape
    depth = w_in.shape[0]
    xt = x.reshape(batch * seq, d)
    for l in range(depth):
        w_main, w_attn = _split_in_proj(w_in[l])
        g1 = norm1_g[l][None, :]
        proj, qkv = _in_proj(xt, g1, w_main, w_attn)
        gates_r = (proj[:, R_PROJ:R_PROJ + 2 * N_HEADS]
                   .reshape(batch * seq // CHUNK, CHUNK, 2 * N_HEADS).transpose(0, 2, 1)
                   .reshape(batch * seq // CHUNK, 2 * N_HEADS * CHUNK))
        y_m, y_r = _recurrent_mixers(
            (proj, gates_r, m_conv_w[l], m_conv_b[l], m_b_i[l], m_b_f[l], m_norm_g[l]),
            (proj, r_mu[l], r_w0[l], r_w_up[l], r_a0[l], r_a_up[l], r_g_up[l],
             r_k_k[l], r_k_a[l], r_r_k[l], r_ln_g[l], r_ln_b[l]),
            batch, seq)
        lam_init = 0.8 - 0.6 * math.exp(-0.3 * l)
        y_a = _diff_attn(qkv, a_lq1[l], a_lk1[l], a_lq2[l], a_lk2[l], a_norm_g[l], lam_init, batch, seq)
        xt = _out_ffn(xt, y_m, y_r, y_a, w_out[l].astype(BF16), norm2_g[l][None, :],
                      w_ff_up[l].astype(BF16), w_ff_down[l].astype(BF16), final_g[None, :],
                      final_norm=(l == depth - 1))
    return xt.reshape(batch, seq, d)
```

```python
import functools
import math

import jax
import jax.numpy as jnp
from jax import lax
from jax.experimental import pallas as pl
from jax.experimental.pallas import tpu as pltpu

F32 = jnp.float32
BF16 = jnp.bfloat16

HEAD_DIM = 64
N_HEADS = 4
M_WIDTH = 256
R_WIDTH = 256
R_PROJ = 896
A_WIDTH = 512
A_PROJ = 1536
CHUNK = 64
ATTN_STRIP = 64
RWKV_TURNS_PER_MLSTM_TURN = 6
ATTN_HEADS_PER_STEP = 4
ATTN_SUM_ROWS = 16
NORM_EPS = 1e-6
M_NORM_EPS = 1e-6
R_LN_EPS = 64e-5
A_NORM_EPS = 1e-5
MAIN_PROJ = 2048
GATE_COL_BLOCK = 7
VMEM_LIMIT = 52 * 1024 * 1024


def _dot(a, b):
    return jnp.dot(a, b, preferred_element_type=F32)


def _dot_nt(a, b):
    return lax.dot_general(a, b, (((1,), (1,)), ((), ())), preferred_element_type=F32)


def _dot_tn(a, b):
    return lax.dot_general(a, b, (((0,), (0,)), ((), ())), preferred_element_type=F32)


def _split3(x):
    hi = x.astype(BF16)
    rem = x - hi.astype(F32)
    mid = rem.astype(BF16)
    lo = (rem - mid.astype(F32)).astype(BF16)
    return hi, mid, lo


def _dot_f32_lhs(a, b_bf16):
    hi, mid, lo = _split3(a)
    return _dot(hi, b_bf16) + _dot(mid, b_bf16) + _dot(lo, b_bf16)


def _dot_f32_rhs(a_bf16, b):
    hi, mid, lo = _split3(b)
    return _dot(a_bf16, hi) + _dot(a_bf16, mid) + _dot(a_bf16, lo)


def _dot_x3(a, b):
    ah = a.astype(BF16)
    bh = b.astype(BF16)
    al = (a - ah.astype(F32)).astype(BF16)
    bl = (b - bh.astype(F32)).astype(BF16)
    return _dot(ah, bh) + _dot(ah, bl) + _dot(al, bh)


def _sigmoid(x):
    return 1.0 / (1.0 + jnp.exp(-x))


def _log_sigmoid(x):
    return jnp.minimum(x, 0.0) - jnp.log(1.0 + jnp.exp(-jnp.abs(x)))


def _iota2(shape, dim):
    return lax.broadcasted_iota(jnp.int32, shape, dim)


def _in_proj_kernel(x_ref, g_ref, wm_ref, wa_ref, wg_ref, om_ref, oa_ref, og_ref):
    x = x_ref[...]
    h = (x * lax.rsqrt(jnp.mean(x * x, axis=-1, keepdims=True) + NORM_EPS) * g_ref[...]).astype(BF16)
    om_ref[...] = _dot(h, wm_ref[...])
    oa_ref[...] = _dot(h, wa_ref[...]).astype(oa_ref.dtype)
    og_ref[...] = _dot_nt(wg_ref[...], h)


def _in_proj(x, g, w_main, w_attn, w_gate_t, tm=512):
    t, d = x.shape
    n_gates = w_gate_t.shape[0]
    rows = lambda i: (i, 0)
    const = lambda i: (0, 0)
    return pl.pallas_call(
        _in_proj_kernel,
        grid=(t // tm,),
        in_specs=[
            pl.BlockSpec((tm, d), rows),
            pl.BlockSpec((1, d), const),
            pl.BlockSpec((d, MAIN_PROJ), const),
            pl.BlockSpec((d, A_PROJ), const),
            pl.BlockSpec((n_gates, d), const),
        ],
        out_specs=[pl.BlockSpec((tm, MAIN_PROJ), rows), pl.BlockSpec((tm, A_PROJ), rows),
                   pl.BlockSpec((n_gates, tm), lambda i: (0, i))],
        out_shape=[jax.ShapeDtypeStruct((t, MAIN_PROJ), F32), jax.ShapeDtypeStruct((t, A_PROJ), BF16),
                   jax.ShapeDtypeStruct((n_gates, t), F32)],
        compiler_params=pltpu.CompilerParams(
            dimension_semantics=("parallel",), vmem_limit_bytes=VMEM_LIMIT),
        name="in_proj",
    )(x, g, w_main, w_attn, w_gate_t)


def _mlstm_kernel(x_ref, gc_ref, gr_ref, cw_ref, cb_ref, bc_ref, br_ref, ng_ref, o_ref,
                  carry_ref, c_ref, n_ref, m_ref, *, tb):
    @pl.when(pl.program_id(1) == 0)
    def _():
        carry_ref[...] = jnp.zeros_like(carry_ref)
        c_ref[...] = jnp.zeros_like(c_ref)
        n_ref[...] = jnp.zeros_like(n_ref)
        m_ref[...] = jnp.zeros_like(m_ref)

    x = x_ref[...]
    qk_pre = x[:, :2 * M_WIDTH]
    ext = jnp.concatenate([carry_ref[...], qk_pre], axis=0)
    carry_ref[...] = qk_pre[tb - 8:, :]
    cw = cw_ref[...]
    acc = cb_ref[...] + cw[3:4, :] * qk_pre
    for j in (1, 2, 3):
        acc = acc + cw[3 - j:4 - j, :] * pltpu.roll(ext, j, axis=0)[8:, :]
    qk = acc * _sigmoid(acc)
    q = qk[:, :M_WIDTH]
    k = qk[:, M_WIDTH:] * (HEAD_DIM ** -0.5)
    v = x[:, 2 * M_WIDTH:3 * M_WIDTH]
    o_gate = _sigmoid(x[:, 3 * M_WIDTH:])

    n = N_HEADS * CHUNK
    row = _iota2((n, n), 0)
    col = _iota2((n, n), 1)
    same_head = (row >> 6) == (col >> 6)
    causal = same_head & ((col & (CHUNK - 1)) <= (row & (CHUNK - 1)))
    eye = row == col
    triu_bd = (same_head & ((row & (CHUNK - 1)) <= (col & (CHUNK - 1)))).astype(BF16)
    tril64 = (_iota2((CHUNK, CHUNK), 1) <= _iota2((CHUNK, CHUNK), 0)).astype(BF16)
    lane_g = _iota2((n, 128), 1)
    head_g = _iota2((n, 128), 0) >> 6
    pick_i = lane_g == head_g
    pick_f = lane_g == head_g + N_HEADS
    ng = ng_ref[...]

    def tile4(z):
        return jnp.concatenate([z, z, z, z], axis=0)

    def col_of(z, pick):
        zz = tile4(z) if z.shape[0] == CHUNK else z
        return jnp.sum(jnp.where(pick, zz, 0.0), axis=-1, keepdims=True)

    def to_row(z_col):
        return jnp.sum(jnp.where(eye, z_col, 0.0), axis=0, keepdims=True)

    gc = gc_ref[...] + bc_ref[...]
    gc = jnp.where(_iota2(gc.shape, 1) < N_HEADS, gc, _log_sigmoid(gc))
    gr = gr_ref[...] + br_ref[...]
    ig_rows = gr[:, :n]
    b_rows = _dot_f32_lhs(_log_sigmoid(gr[:, n:]), triu_bd)

    m_col = m_ref[...]
    pre = []
    for c in range(tb // CHUNK):
        rs = slice(c * CHUNK, (c + 1) * CHUNK)
        gcc = gc[rs, :]
        bc = _dot_f32_rhs(tril64, gcc)
        b_col = col_of(bc, pick_f)
        ig_col = col_of(gcc, pick_i)
        bl_col = col_of(bc[CHUNK - 1:CHUNK, :], pick_f)
        g_end = bl_col - b_col + ig_col
        seg_max = jnp.concatenate(
            [jnp.broadcast_to(jnp.max(g_end[h * CHUNK:(h + 1) * CHUNK, :], axis=0, keepdims=True), (CHUNK, 1))
             for h in range(N_HEADS)], axis=0)
        m_new = jnp.maximum(bl_col + m_col, seg_max)
        pre.append(dict(rs=rs, b_col=b_col, m_col=m_col, wk=jnp.exp(g_end - m_new),
                        cs_col=jnp.exp(bl_col + m_col - m_new),
                        b_row=b_rows[c:c + 1, :], ig_row=ig_rows[c:c + 1, :]))
        m_col = m_new
    m_ref[...] = m_col
    yield

    for ch in pre:
        rs = ch["rs"]
        dmat = jnp.where(causal, ch["b_col"] - ch["b_row"] + ch["ig_row"], -jnp.inf)
        inter = ch["b_col"] + ch["m_col"]
        mt = jnp.maximum(inter, jnp.max(dmat, axis=-1, keepdims=True))
        dw = jnp.exp(dmat - mt)
        iw = jnp.exp(inter - mt)
        q_f = jnp.where(same_head, tile4(q[rs, :]), 0.0)
        k_f = jnp.where(same_head, tile4(k[rs, :]), 0.0)
        q_bd = q_f.astype(BF16)
        v_bd = jnp.where(same_head, tile4(v[rs, :]), 0.0).astype(BF16)
        sqk = _dot_nt(q_bd, tile4(k[rs, :].astype(BF16))) * dw
        cmat = c_ref[...]
        nvec = n_ref[...]
        num = iw * _dot(q_bd, cmat.astype(BF16)) + _dot(sqk.astype(BF16), v_bd)
        den = (iw * jnp.sum(q_f * nvec, axis=-1, keepdims=True)
               + jnp.sum(sqk, axis=-1, keepdims=True))
        hh = num / jnp.maximum(jnp.abs(den), jnp.exp(-mt))
        kw = k_f * ch["wk"]
        c_ref[...] = ch["cs_col"] * cmat + _dot_tn(kw.astype(BF16), v_bd)
        n_ref[...] = to_row(ch["cs_col"]) * nvec + jnp.sum(kw, axis=0, keepdims=True)
        hn = hh * lax.rsqrt(jnp.sum(hh * hh, axis=-1, keepdims=True) * (1.0 / HEAD_DIM) + M_NORM_EPS)
        hc = (hn[0:CHUNK, :] + hn[CHUNK:2 * CHUNK, :] + hn[2 * CHUNK:3 * CHUNK, :] + hn[3 * CHUNK:, :])
        o_ref[rs, :] = (hc * ng * o_gate[rs, :]).astype(o_ref.dtype)
        yield


def _mlstm_parts(proj, gates_r, conv_w, conv_b, b_i, b_f, norm_g, batch, seq, tb):
    nblk = seq // tb
    bias = jnp.concatenate([b_i, b_f]).astype(F32)
    bias_lanes = jnp.pad(bias, (0, 128 - 2 * N_HEADS))[None, :]
    bias_r = jnp.repeat(bias, CHUNK)[None, :]
    const = lambda b, i: (0, 0)
    return dict(
        args=(proj, proj, gates_r, conv_w, conv_b[None, :], bias_lanes, bias_r, norm_g[None, :]),
        in_specs=[
            pl.BlockSpec((tb, 4 * M_WIDTH), lambda b, i: (b * nblk + i, 1)),
            pl.BlockSpec((tb, 128), lambda b, i: (b * nblk + i, GATE_COL_BLOCK)),
            pl.BlockSpec((tb // CHUNK, 2 * N_HEADS * CHUNK), lambda b, i: (b * nblk + i, 0)),
            pl.BlockSpec((4, 2 * M_WIDTH), const),
            pl.BlockSpec((1, 2 * M_WIDTH), const),
            pl.BlockSpec((1, 128), const),
            pl.BlockSpec((1, 2 * N_HEADS * CHUNK), const),
            pl.BlockSpec((1, M_WIDTH), const),
        ],
        out_spec=pl.BlockSpec((tb, M_WIDTH), lambda b, i: (b * nblk + i, 0)),
        out_shape=jax.ShapeDtypeStruct((batch * seq, M_WIDTH), BF16),
        scratch=[
            pltpu.VMEM((8, 2 * M_WIDTH), F32),
            pltpu.VMEM((M_WIDTH, M_WIDTH), F32),
            pltpu.VMEM((1, M_WIDTH), F32),
            pltpu.VMEM((N_HEADS * CHUNK, 1), F32),
        ])


def _rwkv_kernel(p_ref, mu_ref, wl_ref, w0_ref, a0_ref, kk_ref, ka_ref, rk_ref, lng_ref, lnb_ref,
                 o_ref, carry_ref, s_ref, y_ref, *, tb):
    @pl.when(pl.program_id(1) == 0)
    def _():
        carry_ref[...] = jnp.zeros_like(carry_ref)
        s_ref[...] = jnp.zeros_like(s_ref)

    p = p_ref[...]
    ext = jnp.concatenate([carry_ref[...], p], axis=0)
    carry_ref[...] = p[tb - 8:, :]
    prev = pltpu.roll(ext, 1, axis=0)[8:, :]
    pm = p + (prev - p) * mu_ref[...]
    r = pm[:, :R_WIDTH]
    k = pm[:, R_WIDTH:2 * R_WIDTH]
    v = pm[:, 2 * R_WIDTH:3 * R_WIDTH]
    lo = pm[:, 3 * R_WIDTH:]
    lane = _iota2(lo.shape, 1)
    act = jnp.where(lane < 32, jnp.tanh(lo), jnp.where(lane < 64, lo, _sigmoid(lo)))
    lora = _dot_x3(act, wl_ref[...])
    w_log = _log_sigmoid(w0_ref[...] + lora[:, :R_WIDTH]) - 0.5
    lw = -jnp.exp(w_log)
    a = _sigmoid(a0_ref[...] + lora[:, R_WIDTH:2 * R_WIDTH])
    gate = lora[:, 2 * R_WIDTH:]

    n = N_HEADS * CHUNK
    row = _iota2((n, n), 0)
    col = _iota2((n, n), 1)
    same_head = (row >> 6) == (col >> 6)
    t_row = row & (CHUNK - 1)
    t_col = col & (CHUNK - 1)
    incl = same_head & (t_col <= t_row)
    strict = same_head & (t_col < t_row)
    eye = row == col
    eye_f = eye.astype(F32)
    head_ones = same_head.astype(BF16)
    n_levels = int(math.log2(CHUNK))
    level_masks = [
        same_head & ((t_row >> (l + 1)) == (t_col >> (l + 1)))
        & (((t_row >> l) & 1) == 1) & (((t_col >> l) & 1) == 0)
        for l in range(n_levels)
    ]
    tril64 = (_iota2((CHUNK, CHUNK), 1) <= _iota2((CHUNK, CHUNK), 0)).astype(BF16)

    kk = k * kk_ref[...]
    kk = kk / jnp.maximum(jnp.sqrt(_dot_f32_lhs(kk * kk, head_ones)), 1e-12)
    k2 = k * (1.0 + (a - 1.0) * ka_ref[...])
    bvec = kk * a

    def tile4(z):
        return jnp.concatenate([z, z, z, z], axis=0)

    def block_diag(z):
        return jnp.where(same_head, tile4(z), 0.0).astype(BF16)

    chunks = []
    for c in range(tb // CHUNK):
        rs = slice(c * CHUNK, (c + 1) * CHUNK)
        lwc = lw[rs, :]
        g = _dot_f32_rhs(tril64, lwc)
        g_last = g[CHUNK - 1:CHUNK, :]
        e_pos = jnp.exp(g)
        e_neg = jnp.exp(-g)
        e_end = jnp.exp(g_last - g)
        e_last = jnp.exp(g_last)
        ar = jnp.concatenate([block_diag(-kk[rs, :] * jnp.exp(g - lwc)),
                              block_diag(r[rs, :] * e_pos)], axis=0)
        bk = jnp.concatenate([tile4((bvec[rs, :] * e_neg).astype(BF16)),
                              tile4((k2[rs, :] * e_neg).astype(BF16))], axis=0)
        bkh = jnp.concatenate([tile4((bvec[rs, :] * e_end).astype(BF16)),
                               tile4((k2[rs, :] * e_end).astype(BF16))], axis=0)
        v_bd = block_diag(v[rs, :])
        mm = _dot_nt(ar, bk)
        ab = jnp.where(strict, mm[:n, :n], 0.0)
        ak = jnp.where(strict, mm[:n, n:], 0.0).astype(BF16)
        rbk = jnp.concatenate([jnp.where(incl, mm[n:, :n], 0.0).astype(BF16),
                               jnp.where(incl, mm[n:, n:], 0.0).astype(BF16)], axis=1)
        e_last_col = jnp.sum(jnp.where(eye, e_last, 0.0), axis=-1, keepdims=True)
        chunks.append(dict(rs=rs, ar=ar, bkh=bkh, v_bd=v_bd, ab=ab, rbk=rbk, akv=_dot(ak, v_bd),
                           e_last_col=e_last_col, tinv=eye_f + jnp.where(level_masks[0], ab, 0.0)))
        yield
    for l in range(1, n_levels):
        for ch in chunks:
            t_bf = ch["tinv"].astype(BF16)
            join = jnp.where(level_masks[l], ch["ab"], 0.0).astype(BF16)
            ch["tinv"] = ch["tinv"] + _dot(t_bf, _dot(join, t_bf).astype(BF16))
            yield
    for ch in chunks:
        h0 = s_ref[...]
        ah = _dot(ch["ar"], h0.astype(BF16))
        u = _dot(ch["tinv"].astype(BF16), (ah[:n, :] + ch["akv"]).astype(BF16))
        uv = jnp.concatenate([u.astype(BF16), ch["v_bd"]], axis=0)
        o = ah[n:, :] + _dot(ch["rbk"], uv)
        y_ref[ch["rs"], :] = (o[0:CHUNK, :] + o[CHUNK:2 * CHUNK, :]
                              + o[2 * CHUNK:3 * CHUNK, :] + o[3 * CHUNK:, :])
        s_ref[...] = h0 * ch["e_last_col"] + jnp.where(same_head, _dot_tn(ch["bkh"], uv), 0.0)
        yield

    y = y_ref[...]
    inv_d = 1.0 / HEAD_DIM
    mean = _dot_f32_lhs(y, head_ones) * inv_d
    yc = y - mean
    var = _dot_f32_lhs(yc * yc, head_ones) * inv_d
    yn = yc * lax.rsqrt(var + R_LN_EPS) * lng_ref[...] + lnb_ref[...]
    bonus = _dot_f32_lhs(r * k2 * rk_ref[...], head_ones)
    o_ref[...] = ((yn + bonus * v) * gate).astype(o_ref.dtype)


def _rwkv_parts(proj, mu, w0, w_up, a0, a_up, g_up, k_k, k_a, r_k, ln_g, ln_b, batch, seq, tb):
    nblk = seq // tb
    w_lora = jnp.zeros((128, 3 * R_WIDTH), F32)
    w_lora = w_lora.at[0:32, 0:R_WIDTH].set(w_up)
    w_lora = w_lora.at[32:64, R_WIDTH:2 * R_WIDTH].set(a_up)
    w_lora = w_lora.at[64:128, 2 * R_WIDTH:].set(g_up)
    const = lambda b, i: (0, 0)
    vec = pl.BlockSpec((1, R_WIDTH), const)
    return dict(
        args=(proj, mu[None, :], w_lora, w0[None, :], a0[None, :], k_k[None, :], k_a[None, :],
              r_k.reshape(1, R_WIDTH), ln_g[None, :], ln_b[None, :]),
        in_specs=[
            pl.BlockSpec((tb, R_PROJ), lambda b, i: (b * nblk + i, 0)),
            pl.BlockSpec((1, R_PROJ), const),
            pl.BlockSpec((128, 3 * R_WIDTH), const),
            vec, vec, vec, vec, vec, vec, vec,
        ],
        out_spec=pl.BlockSpec((tb, R_WIDTH), lambda b, i: (b * nblk + i, 0)),
        out_shape=jax.ShapeDtypeStruct((batch * seq, R_WIDTH), BF16),
        scratch=[
            pltpu.VMEM((8, R_PROJ), F32),
            pltpu.VMEM((R_WIDTH, R_WIDTH), F32),
            pltpu.VMEM((tb, R_WIDTH), F32),
        ])


def _recurrent_mixers(mlstm_args, rwkv_args, batch, seq, tb=512):
    mp = _mlstm_parts(*mlstm_args, batch, seq, tb)
    rp = _rwkv_parts(*rwkv_args, batch, seq, tb)
    n_m_in, n_r_in = len(mp["args"]), len(rp["args"])
    n_m_scr = len(mp["scratch"])

    def kern(*refs):
        m_in = refs[:n_m_in]
        r_in = refs[n_m_in:n_m_in + n_r_in]
        om_ref, or_ref = refs[n_m_in + n_r_in:n_m_in + n_r_in + 2]
        scr = refs[n_m_in + n_r_in + 2:]
        rwkv = _rwkv_kernel(*r_in, or_ref, *scr[n_m_scr:], tb=tb)
        mlstm = _mlstm_kernel(*m_in, om_ref, *scr[:n_m_scr], tb=tb)
        live = [rwkv, mlstm]
        while live:
            for gen, turns in ((rwkv, RWKV_TURNS_PER_MLSTM_TURN), (mlstm, 1)):
                for _ in range(turns):
                    if gen in live and next(gen, "done") == "done":
                        live.remove(gen)

    return pl.pallas_call(
        kern,
        grid=(batch, seq // tb),
        in_specs=mp["in_specs"] + rp["in_specs"],
        out_specs=[mp["out_spec"], rp["out_spec"]],
        out_shape=[mp["out_shape"], rp["out_shape"]],
        scratch_shapes=mp["scratch"] + rp["scratch"],
        compiler_params=pltpu.CompilerParams(
            dimension_semantics=("parallel", "arbitrary"), vmem_limit_bytes=VMEM_LIMIT),
        name="recurrent_mixers",
    )(*mp["args"], *rp["args"])


def _attn_kernel(lq1_ref, lk1_ref, lq2_ref, lk2_ref, g_ref, q_ref, k_ref, v_ref, o_ref,
                 vt_ref, sa_ref, sb_ref, p_ref, m_ref, acc_ref, *, tq, lam_init):
    qi = pl.program_id(2)
    dv = 2 * HEAD_DIM
    heads = range(ATTN_HEADS_PER_STEP)
    streams = range(2 * ATTN_HEADS_PER_STEP)

    @pl.when(qi == 0)
    def _():
        for hh in heads:
            for jj in range(vt_ref.shape[1]):
                v_blk = v_ref[jj * tq:(jj + 1) * tq, hh * dv:(hh + 1) * dv]
                vt_ref[hh, jj, :dv, :] = v_blk.astype(F32).T.astype(BF16)
                vt_ref[hh, jj, dv:, :] = jnp.ones((ATTN_SUM_ROWS, tq), BF16)

    lam = (jnp.exp(jnp.sum(lq1_ref[...] * lk1_ref[...], axis=-1, keepdims=True))
           - jnp.exp(jnp.sum(lq2_ref[...] * lk2_ref[...], axis=-1, keepdims=True)) + lam_init)
    lane = _iota2((tq, dv), 1)
    q_halves = []
    for hh in heads:
        q = q_ref[:, hh * dv:(hh + 1) * dv]
        zero = jnp.zeros_like(q)
        q_halves += [jnp.where(lane < HEAD_DIM, q, zero), jnp.where(lane >= HEAD_DIM, q, zero)]
    strips = [slice(r * ATTN_STRIP, (r + 1) * ATTN_STRIP) for r in range(tq // ATTN_STRIP)]
    groups = ATTN_STRIP // 8

    def put_scores(j, dst_ref):
        start = pl.multiple_of(j * tq, tq)
        for hh in heads:
            kb = k_ref[pl.ds(start, tq), hh * dv:(hh + 1) * dv]
            for u in (2 * hh, 2 * hh + 1):
                dst_ref[u] = _dot_nt(kb, q_halves[u])

    def strip_scores(src_ref, i, rows, masked):
        s = src_ref[i, rows, :]
        if masked:
            keep = _iota2((ATTN_STRIP, tq), 0) + rows.start <= _iota2((ATTN_STRIP, tq), 1)
            s = jnp.where(keep, s, -jnp.inf)
        return s.reshape(groups, 8, tq)

    def update(j, src_ref, masked):
        for i in streams:
            m = m_ref[i]
            mx8 = None
            for rows in strips:
                t = jnp.max(strip_scores(src_ref, i, rows, masked), axis=0)
                mx8 = t if mx8 is None else jnp.maximum(mx8, t)
            m_new = jnp.maximum(m, jnp.max(mx8, axis=0, keepdims=True))
            for rows in strips:
                pr = jnp.exp(strip_scores(src_ref, i, rows, masked) - m_new)
                p_ref[i, rows, :] = pr.reshape(ATTN_STRIP, tq).astype(BF16)
            m_ref[i] = m_new
            acc_ref[i] = jnp.exp(m - m_new) * acc_ref[i] + _dot(vt_ref[i // 2, j], p_ref[i])

    m_ref[...] = jnp.full(m_ref.shape, -1e30, F32)
    acc_ref[...] = jnp.zeros_like(acc_ref)

    def body(t, carry):
        j = 2 * t
        put_scores(j + 1, sb_ref)
        update(j, sa_ref, False)
        put_scores(j + 2, sa_ref)
        update(j + 1, sb_ref, False)
        return carry

    put_scores(0, sa_ref)
    lax.fori_loop(0, qi // 2, body, 0)

    @pl.when(qi % 2 == 1)
    def _():
        put_scores(qi, sb_ref)
        update(qi - 1, sa_ref, False)
        update(qi, sb_ref, True)

    @pl.when(qi % 2 == 0)
    def _():
        update(qi, sa_ref, True)

    for hh in heads:
        a1, a2 = acc_ref[2 * hh], acc_ref[2 * hh + 1]
        o = (a1[:dv, :] / a1[dv:dv + 1, :] - lam * (a2[:dv, :] / a2[dv:dv + 1, :])).T
        o = o * lax.rsqrt(jnp.mean(o * o, axis=-1, keepdims=True) + A_NORM_EPS) * g_ref[:, hh * dv:(hh + 1) * dv]
        o_ref[:, hh * dv:(hh + 1) * dv] = (o * (1.0 - lam_init)).astype(o_ref.dtype)


def _diff_attn(qkv, lq1, lk1, lq2, lk2, norm_g, lam_init, batch, seq, tq=512):
    nq = seq // tq
    nh = ATTN_HEADS_PER_STEP
    n_groups = N_HEADS // nh
    wide = nh * 2 * HEAD_DIM
    kern = functools.partial(_attn_kernel, tq=tq, lam_init=lam_init)
    const = lambda b, g, i: (0, 0)
    lvec = pl.BlockSpec((1, HEAD_DIM), const)
    return pl.pallas_call(
        kern,
        grid=(batch, n_groups, nq),
        in_specs=[
            lvec, lvec, lvec, lvec,
            pl.BlockSpec((1, wide), lambda b, g, i: (0, g)),
            pl.BlockSpec((tq, wide), lambda b, g, i: (b * nq + i, g)),
            pl.BlockSpec((seq, wide), lambda b, g, i: (b, n_groups + g)),
            pl.BlockSpec((seq, wide), lambda b, g, i: (b, 2 * n_groups + g)),
        ],
        out_specs=pl.BlockSpec((tq, wide), lambda b, g, i: (b * nq + i, g)),
        out_shape=jax.ShapeDtypeStruct((batch * seq, A_WIDTH), BF16),
        scratch_shapes=[
            pltpu.VMEM((nh, nq, 2 * HEAD_DIM + ATTN_SUM_ROWS, tq), BF16),
            pltpu.VMEM((2 * nh, tq, tq), F32),
            pltpu.VMEM((2 * nh, tq, tq), F32),
            pltpu.VMEM((2 * nh, tq, tq), BF16),
            pltpu.VMEM((2 * nh, 1, tq), F32),
            pltpu.VMEM((2 * nh, 2 * HEAD_DIM + ATTN_SUM_ROWS, tq), F32),
        ],
        compiler_params=pltpu.CompilerParams(
            dimension_semantics=("parallel", "parallel", "arbitrary"), vmem_limit_bytes=VMEM_LIMIT),
        name="diff_attn",
    )(lq1[None, :], lk1[None, :], lq2[None, :], lk2[None, :], norm_g[None, :], qkv, qkv, qkv)


def _out_ffn_kernel(x_ref, ym_ref, yr_ref, ya_ref, wo_ref, g2_ref, wu_ref, wd_ref, gf_ref, o_ref,
                    h_ref, acc_ref, *, final_norm):
    j = pl.program_id(1)

    @pl.when(j == 0)
    def _():
        mix = jnp.concatenate([ym_ref[...], yr_ref[...], ya_ref[...]], axis=1)
        xn = x_ref[...] + _dot(mix, wo_ref[...])
        acc_ref[...] = xn
        hn = xn * lax.rsqrt(jnp.mean(xn * xn, axis=-1, keepdims=True) + NORM_EPS) * g2_ref[...]
        h_ref[...] = hn.astype(BF16)

    u = jnp.maximum(_dot(h_ref[...], wu_ref[...]), 0.0)
    acc_ref[...] += _dot((u * u).astype(BF16), wd_ref[...])

    @pl.when(j == pl.num_programs(1) - 1)
    def _():
        y = acc_ref[...]
        if final_norm:
            y = y * lax.rsqrt(jnp.mean(y * y, axis=-1, keepdims=True) + NORM_EPS) * gf_ref[...]
        o_ref[...] = y


def _out_ffn(x, ym, yr, ya, w_out, g2, w_up, w_down, g_final, final_norm, tm=1024, tf=1024):
    t, d = x.shape
    f = w_up.shape[1]
    kern = functools.partial(_out_ffn_kernel, final_norm=final_norm)
    rows = lambda i, j: (i, 0)
    const = lambda i, j: (0, 0)
    return pl.pallas_call(
        kern,
        grid=(t // tm, f // tf),
        in_specs=[
            pl.BlockSpec((tm, d), rows),
            pl.BlockSpec((tm, M_WIDTH), rows),
            pl.BlockSpec((tm, R_WIDTH), rows),
            pl.BlockSpec((tm, A_WIDTH), rows),
            pl.BlockSpec((d, d), const),
            pl.BlockSpec((1, d), const),
            pl.BlockSpec((d, tf), lambda i, j: (0, j)),
            pl.BlockSpec((tf, d), lambda i, j: (j, 0)),
            pl.BlockSpec((1, d), const),
        ],
        out_specs=pl.BlockSpec((tm, d), rows),
        out_shape=jax.ShapeDtypeStruct((t, d), F32),
        scratch_shapes=[pltpu.VMEM((tm, d), BF16), pltpu.VMEM((tm, d), F32)],
        compiler_params=pltpu.CompilerParams(
            dimension_semantics=("parallel", "arbitrary"), vmem_limit_bytes=VMEM_LIMIT),
        name="out_ffn",
    )(x, ym, yr, ya, w_out, g2, w_up, w_down, g_final)


def _split_in_proj(w):
    m_main = w[:, 0:4 * M_WIDTH]
    m_gate = w[:, 4 * M_WIDTH:4 * M_WIDTH + 2 * N_HEADS]
    r0 = 4 * M_WIDTH + 2 * N_HEADS
    r_all = w[:, r0:r0 + R_PROJ]
    a0 = r0 + R_PROJ
    a_q = w[:, a0:a0 + A_WIDTH] * (HEAD_DIM ** -0.5)
    a_kv = w[:, a0 + A_WIDTH:a0 + A_PROJ]
    pad = jnp.zeros((w.shape[0], 128 - 2 * N_HEADS), w.dtype)
    w_main = jnp.concatenate([r_all, m_gate, pad, m_main], axis=1).astype(BF16)
    w_attn = jnp.concatenate([a_q, a_kv], axis=1).astype(BF16)
    return w_main, w_attn, m_gate.T.astype(BF16)


def kernel(x, norm1_g, w_in, m_conv_w, m_conv_b, m_b_i, m_b_f, m_norm_g, r_mu, r_w0, r_w_up, r_a0, r_a_up, r_g_up, r_k_k, r_k_a, r_r_k, r_ln_g, r_ln_b, a_lq1, a_lk1, a_lq2, a_lk2, a_norm_g, w_out, norm2_g, w_ff_up, w_ff_down, final_g):
    batch, seq, d = x.shape
    depth = w_in.shape[0]
    xt = x.reshape(batch * seq, d)
    for l in range(depth):
        w_main, w_attn, w_gate_t = _split_in_proj(w_in[l])
        g1 = norm1_g[l][None, :]
        proj, qkv, gates_t = _in_proj(xt, g1, w_main, w_attn, w_gate_t)
        gates_r = (gates_t.reshape(2 * N_HEADS, batch * seq // CHUNK, CHUNK).transpose(1, 0, 2)
                   .reshape(batch * seq // CHUNK, 2 * N_HEADS * CHUNK))
        y_m, y_r = _recurrent_mixers(
            (proj, gates_r, m_conv_w[l], m_conv_b[l], m_b_i[l], m_b_f[l], m_norm_g[l]),
            (proj, r_mu[l], r_w0[l], r_w_up[l], r_a0[l], r_a_up[l], r_g_up[l],
             r_k_k[l], r_k_a[l], r_r_k[l], r_ln_g[l], r_ln_b[l]),
            batch, seq)
        lam_init = 0.8 - 0.6 * math.exp(-0.3 * l)
        y_a = _diff_attn(qkv, a_lq1[l], a_lk1[l], a_lq2[l], a_lk2[l], a_norm_g[l], lam_init, batch, seq)
        xt = _out_ffn(xt, y_m, y_r, y_a, w_out[l].astype(BF16), norm2_g[l][None, :],
                      w_ff_up[l].astype(BF16), w_ff_down[l].astype(BF16), final_g[None, :],
                      final_norm=(l == depth - 1))
    return xt.reshape(batch, seq, d)
```

```python
import functools
import math

import jax
import jax.numpy as jnp
from jax import lax
from jax.experimental import pallas as pl
from jax.experimental.pallas import tpu as pltpu

F32 = jnp.float32
BF16 = jnp.bfloat16

HEAD_DIM = 64
N_HEADS = 4
M_WIDTH = 256
R_WIDTH = 256
R_PROJ = 896
A_WIDTH = 512
A_PROJ = 1536
CHUNK = 64
ATTN_STRIP = 64
RWKV_GROUP_HEADS = 2
MLSTM_GROUP_HEADS = 2
RWKV_HEAD_START = 56
RWKV_TURNS_PER_MLSTM_TURN = 56
ATTN_HEADS_PER_STEP = 4
ATTN_SUM_ROWS = 16
NORM_EPS = 1e-6
M_NORM_EPS = 1e-6
R_LN_EPS = 64e-5
A_NORM_EPS = 1e-5
MAIN_PROJ = 2048
GATE_COL_BLOCK = 7
VMEM_LIMIT = 52 * 1024 * 1024


def _dot(a, b):
    return jnp.dot(a, b, preferred_element_type=F32)


def _dot_nt(a, b):
    return lax.dot_general(a, b, (((1,), (1,)), ((), ())), preferred_element_type=F32)


def _dot_tn(a, b):
    return lax.dot_general(a, b, (((0,), (0,)), ((), ())), preferred_element_type=F32)


def _split3(x):
    hi = x.astype(BF16)
    rem = x - hi.astype(F32)
    mid = rem.astype(BF16)
    lo = (rem - mid.astype(F32)).astype(BF16)
    return hi, mid, lo


def _dot_f32_lhs(a, b_bf16):
    hi, mid, lo = _split3(a)
    return _dot(hi, b_bf16) + _dot(mid, b_bf16) + _dot(lo, b_bf16)


def _dot_f32_rhs(a_bf16, b):
    hi, mid, lo = _split3(b)
    return _dot(a_bf16, hi) + _dot(a_bf16, mid) + _dot(a_bf16, lo)


def _dot_x3(a, b):
    ah = a.astype(BF16)
    bh = b.astype(BF16)
    al = (a - ah.astype(F32)).astype(BF16)
    bl = (b - bh.astype(F32)).astype(BF16)
    return _dot(ah, bh) + _dot(ah, bl) + _dot(al, bh)


def _sigmoid(x):
    return 1.0 / (1.0 + jnp.exp(-x))


def _log_sigmoid(x):
    return jnp.minimum(x, 0.0) - jnp.log(1.0 + jnp.exp(-jnp.abs(x)))


def _iota2(shape, dim):
    return lax.broadcasted_iota(jnp.int32, shape, dim)


def _in_proj_kernel(x_ref, g_ref, wm_ref, wa_ref, wg_ref, om_ref, oa_ref, og_ref):
    x = x_ref[...]
    h = (x * lax.rsqrt(jnp.mean(x * x, axis=-1, keepdims=True) + NORM_EPS) * g_ref[...]).astype(BF16)
    om_ref[...] = _dot(h, wm_ref[...])
    oa_ref[...] = _dot(h, wa_ref[...]).astype(oa_ref.dtype)
    og_ref[...] = _dot_nt(wg_ref[...], h)


def _in_proj(x, g, w_main, w_attn, w_gate_t, tm=512):
    t, d = x.shape
    n_gates = w_gate_t.shape[0]
    rows = lambda i: (i, 0)
    const = lambda i: (0, 0)
    return pl.pallas_call(
        _in_proj_kernel,
        grid=(t // tm,),
        in_specs=[
            pl.BlockSpec((tm, d), rows),
            pl.BlockSpec((1, d), const),
            pl.BlockSpec((d, MAIN_PROJ), const),
            pl.BlockSpec((d, A_PROJ), const),
            pl.BlockSpec((n_gates, d), const),
        ],
        out_specs=[pl.BlockSpec((tm, MAIN_PROJ), rows), pl.BlockSpec((tm, A_PROJ), rows),
                   pl.BlockSpec((n_gates, tm), lambda i: (0, i))],
        out_shape=[jax.ShapeDtypeStruct((t, MAIN_PROJ), F32), jax.ShapeDtypeStruct((t, A_PROJ), BF16),
                   jax.ShapeDtypeStruct((n_gates, t), F32)],
        compiler_params=pltpu.CompilerParams(
            dimension_semantics=("parallel",), vmem_limit_bytes=VMEM_LIMIT),
        name="in_proj",
    )(x, g, w_main, w_attn, w_gate_t)


def _mlstm_kernel(x_ref, gc_ref, gr_ref, cw_ref, cb_ref, bc_ref, br_ref, ng_ref, o_ref,
                  carry_ref, c_ref, n_ref, m_ref, *, tb):
    @pl.when(pl.program_id(1) == 0)
    def _():
        carry_ref[...] = jnp.zeros_like(carry_ref)
        c_ref[...] = jnp.zeros_like(c_ref)
        n_ref[...] = jnp.zeros_like(n_ref)
        m_ref[...] = jnp.zeros_like(m_ref)

    x = x_ref[...]
    qk_pre = x[:, :2 * M_WIDTH]
    ext = jnp.concatenate([carry_ref[...], qk_pre], axis=0)
    carry_ref[...] = qk_pre[tb - 8:, :]
    cw = cw_ref[...]
    acc = cb_ref[...] + cw[3:4, :] * qk_pre
    for j in (1, 2, 3):
        acc = acc + cw[3 - j:4 - j, :] * pltpu.roll(ext, j, axis=0)[8:, :]
    qk = acc * _sigmoid(acc)
    q = qk[:, :M_WIDTH]
    k = qk[:, M_WIDTH:] * (HEAD_DIM ** -0.5)
    v = x[:, 2 * M_WIDTH:3 * M_WIDTH]
    o_gate = _sigmoid(x[:, 3 * M_WIDTH:])

    n = MLSTM_GROUP_HEADS * CHUNK
    n_groups = N_HEADS // MLSTM_GROUP_HEADS
    row = _iota2((n, n), 0)
    col = _iota2((n, n), 1)
    same_head = (row >> 6) == (col >> 6)
    causal = same_head & ((col & (CHUNK - 1)) <= (row & (CHUNK - 1)))
    eye = row == col
    full = N_HEADS * CHUNK
    row_f = _iota2((full, full), 0)
    col_f = _iota2((full, full), 1)
    triu_bd = (((row_f >> 6) == (col_f >> 6))
               & ((row_f & (CHUNK - 1)) <= (col_f & (CHUNK - 1)))).astype(BF16)
    tril64 = (_iota2((CHUNK, CHUNK), 1) <= _iota2((CHUNK, CHUNK), 0)).astype(BF16)
    lane_g = _iota2((n, 128), 1)
    head_g = _iota2((n, 128), 0) >> 6
    pick_i = [lane_g == head_g + grp * MLSTM_GROUP_HEADS for grp in range(n_groups)]
    pick_f = [lane_g == head_g + grp * MLSTM_GROUP_HEADS + N_HEADS for grp in range(n_groups)]
    ng = ng_ref[...]

    def tiled(z):
        return jnp.concatenate([z] * MLSTM_GROUP_HEADS, axis=0)

    def col_of(z, pick):
        zz = tiled(z) if z.shape[0] == CHUNK else z
        return jnp.sum(jnp.where(pick, zz, 0.0), axis=-1, keepdims=True)

    def to_row(z_col):
        return jnp.sum(jnp.where(eye, z_col, 0.0), axis=0, keepdims=True)

    gc = gc_ref[...] + bc_ref[...]
    gc = jnp.where(_iota2(gc.shape, 1) < N_HEADS, gc, _log_sigmoid(gc))
    gr = gr_ref[...] + br_ref[...]
    ig_rows = gr[:, :full]
    b_rows = _dot_f32_lhs(_log_sigmoid(gr[:, full:]), triu_bd)

    m_cols = [m_ref[grp] for grp in range(n_groups)]
    pre = []
    for c in range(tb // CHUNK):
        rs = slice(c * CHUNK, (c + 1) * CHUNK)
        gcc = gc[rs, :]
        bc = _dot_f32_rhs(tril64, gcc)
        for grp in range(n_groups):
            cs = slice(grp * n, (grp + 1) * n)
            m_col = m_cols[grp]
            b_col = col_of(bc, pick_f[grp])
            ig_col = col_of(gcc, pick_i[grp])
            bl_col = col_of(bc[CHUNK - 1:CHUNK, :], pick_f[grp])
            g_end = bl_col - b_col + ig_col
            seg_max = jnp.concatenate(
                [jnp.broadcast_to(jnp.max(g_end[h * CHUNK:(h + 1) * CHUNK, :], axis=0, keepdims=True),
                                  (CHUNK, 1)) for h in range(MLSTM_GROUP_HEADS)], axis=0)
            m_new = jnp.maximum(bl_col + m_col, seg_max)
            pre.append(dict(rs=rs, cs=cs, grp=grp, b_col=b_col, m_col=m_col, wk=jnp.exp(g_end - m_new),
                            cs_col=jnp.exp(bl_col + m_col - m_new),
                            b_row=b_rows[c:c + 1, cs], ig_row=ig_rows[c:c + 1, cs]))
            m_cols[grp] = m_new
    for grp in range(n_groups):
        m_ref[grp] = m_cols[grp]
    yield

    for ch in pre:
        rs, cs, grp = ch["rs"], ch["cs"], ch["grp"]
        dmat = jnp.where(causal, ch["b_col"] - ch["b_row"] + ch["ig_row"], -jnp.inf)
        inter = ch["b_col"] + ch["m_col"]
        mt = jnp.maximum(inter, jnp.max(dmat, axis=-1, keepdims=True))
        dw = jnp.exp(dmat - mt)
        iw = jnp.exp(inter - mt)
        q_f = jnp.where(same_head, tiled(q[rs, cs]), 0.0)
        k_f = jnp.where(same_head, tiled(k[rs, cs]), 0.0)
        q_bd = q_f.astype(BF16)
        v_bd = jnp.where(same_head, tiled(v[rs, cs]), 0.0).astype(BF16)
        sqk = _dot_nt(q_bd, tiled(k[rs, cs].astype(BF16))) * dw
        cmat = c_ref[grp]
        nvec = n_ref[:, cs]
        num = iw * _dot(q_bd, cmat.astype(BF16)) + _dot(sqk.astype(BF16), v_bd)
        den = (iw * jnp.sum(q_f * nvec, axis=-1, keepdims=True)
               + jnp.sum(sqk, axis=-1, keepdims=True))
        hh = num / jnp.maximum(jnp.abs(den), jnp.exp(-mt))
        kw = k_f * ch["wk"]
        c_ref[grp] = ch["cs_col"] * cmat + _dot_tn(kw.astype(BF16), v_bd)
        n_ref[:, cs] = to_row(ch["cs_col"]) * nvec + jnp.sum(kw, axis=0, keepdims=True)
        hn = hh * lax.rsqrt(jnp.sum(hh * hh, axis=-1, keepdims=True) * (1.0 / HEAD_DIM) + M_NORM_EPS)
        hc = hn[0:CHUNK, :] + hn[CHUNK:, :]
        o_ref[rs, cs] = (hc * ng[:, cs] * o_gate[rs, cs]).astype(o_ref.dtype)
        yield


def _mlstm_parts(proj, gates_r, conv_w, conv_b, b_i, b_f, norm_g, batch, seq, tb):
    nblk = seq // tb
    bias = jnp.concatenate([b_i, b_f]).astype(F32)
    bias_lanes = jnp.pad(bias, (0, 128 - 2 * N_HEADS))[None, :]
    bias_r = jnp.repeat(bias, CHUNK)[None, :]
    const = lambda b, i: (0, 0)
    return dict(
        args=(proj, proj, gates_r, conv_w, conv_b[None, :], bias_lanes, bias_r, norm_g[None, :]),
        in_specs=[
            pl.BlockSpec((tb, 4 * M_WIDTH), lambda b, i: (b * nblk + i, 1)),
            pl.BlockSpec((tb, 128), lambda b, i: (b * nblk + i, GATE_COL_BLOCK)),
            pl.BlockSpec((tb // CHUNK, 2 * N_HEADS * CHUNK), lambda b, i: (b * nblk + i, 0)),
            pl.BlockSpec((4, 2 * M_WIDTH), const),
            pl.BlockSpec((1, 2 * M_WIDTH), const),
            pl.BlockSpec((1, 128), const),
            pl.BlockSpec((1, 2 * N_HEADS * CHUNK), const),
            pl.BlockSpec((1, M_WIDTH), const),
        ],
        out_spec=pl.BlockSpec((tb, M_WIDTH), lambda b, i: (b * nblk + i, 0)),
        out_shape=jax.ShapeDtypeStruct((batch * seq, M_WIDTH), BF16),
        scratch=[
            pltpu.VMEM((8, 2 * M_WIDTH), F32),
            pltpu.VMEM((N_HEADS // MLSTM_GROUP_HEADS, MLSTM_GROUP_HEADS * HEAD_DIM, MLSTM_GROUP_HEADS * HEAD_DIM), F32),
            pltpu.VMEM((1, M_WIDTH), F32),
            pltpu.VMEM((N_HEADS // MLSTM_GROUP_HEADS, MLSTM_GROUP_HEADS * CHUNK, 1), F32),
        ])


def _rwkv_kernel(p_ref, mu_ref, wl_ref, w0_ref, a0_ref, kk_ref, ka_ref, rk_ref, lng_ref, lnb_ref,
                 o_ref, carry_ref, s_ref, y_ref, *, tb):
    @pl.when(pl.program_id(1) == 0)
    def _():
        carry_ref[...] = jnp.zeros_like(carry_ref)
        s_ref[...] = jnp.zeros_like(s_ref)

    p = p_ref[...]
    ext = jnp.concatenate([carry_ref[...], p], axis=0)
    carry_ref[...] = p[tb - 8:, :]
    prev = pltpu.roll(ext, 1, axis=0)[8:, :]
    pm = p + (prev - p) * mu_ref[...]
    r = pm[:, :R_WIDTH]
    k = pm[:, R_WIDTH:2 * R_WIDTH]
    v = pm[:, 2 * R_WIDTH:3 * R_WIDTH]
    lo = pm[:, 3 * R_WIDTH:]
    lane = _iota2(lo.shape, 1)
    act = jnp.where(lane < 32, jnp.tanh(lo), jnp.where(lane < 64, lo, _sigmoid(lo)))
    lora = _dot_x3(act, wl_ref[...])
    w_log = _log_sigmoid(w0_ref[...] + lora[:, :R_WIDTH]) - 0.5
    lw = -jnp.exp(w_log)
    a = _sigmoid(a0_ref[...] + lora[:, R_WIDTH:2 * R_WIDTH])
    gate = lora[:, 2 * R_WIDTH:]

    n = RWKV_GROUP_HEADS * CHUNK
    row = _iota2((n, n), 0)
    col = _iota2((n, n), 1)
    same_head = (row >> 6) == (col >> 6)
    t_row = row & (CHUNK - 1)
    t_col = col & (CHUNK - 1)
    incl = same_head & (t_col <= t_row)
    strict = same_head & (t_col < t_row)
    eye = row == col
    eye_f = eye.astype(F32)
    head_ones = ((_iota2((R_WIDTH, R_WIDTH), 0) >> 6) == (_iota2((R_WIDTH, R_WIDTH), 1) >> 6)).astype(BF16)
    n_levels = int(math.log2(CHUNK))
    level_masks = [
        same_head & ((t_row >> (l + 1)) == (t_col >> (l + 1)))
        & (((t_row >> l) & 1) == 1) & (((t_col >> l) & 1) == 0)
        for l in range(n_levels)
    ]
    tril64 = (_iota2((CHUNK, CHUNK), 1) <= _iota2((CHUNK, CHUNK), 0)).astype(BF16)

    kk = k * kk_ref[...]
    kk = kk / jnp.maximum(jnp.sqrt(_dot_f32_lhs(kk * kk, head_ones)), 1e-12)
    k2 = k * (1.0 + (a - 1.0) * ka_ref[...])
    bvec = kk * a

    def tiled(z):
        return jnp.concatenate([z] * RWKV_GROUP_HEADS, axis=0)

    def block_diag(z):
        return jnp.where(same_head, tiled(z), 0.0).astype(BF16)

    units = []
    for c in range(tb // CHUNK):
        rs = slice(c * CHUNK, (c + 1) * CHUNK)
        lwc = lw[rs, :]
        g = _dot_f32_rhs(tril64, lwc)
        g_last = g[CHUNK - 1:CHUNK, :]
        e_pos = jnp.exp(g)
        e_neg = jnp.exp(-g)
        e_end = jnp.exp(g_last - g)
        e_last = jnp.exp(g_last)
        a_t = -kk[rs, :] * jnp.exp(g - lwc)
        r_t = r[rs, :] * e_pos
        b_t = (bvec[rs, :] * e_neg).astype(BF16)
        k_t = (k2[rs, :] * e_neg).astype(BF16)
        b_h = (bvec[rs, :] * e_end).astype(BF16)
        k_h = (k2[rs, :] * e_end).astype(BF16)
        for grp in range(N_HEADS // RWKV_GROUP_HEADS):
            cs = slice(grp * n, (grp + 1) * n)
            ar = jnp.concatenate([block_diag(a_t[:, cs]), block_diag(r_t[:, cs])], axis=0)
            bk = jnp.concatenate([tiled(b_t[:, cs]), tiled(k_t[:, cs])], axis=0)
            bkh = jnp.concatenate([tiled(b_h[:, cs]), tiled(k_h[:, cs])], axis=0)
            v_bd = block_diag(v[rs, cs])
            mm = _dot_nt(ar, bk)
            ab = jnp.where(strict, mm[:n, :n], 0.0)
            ak = jnp.where(strict, mm[:n, n:], 0.0).astype(BF16)
            rbk = jnp.concatenate([jnp.where(incl, mm[n:, :n], 0.0).astype(BF16),
                                   jnp.where(incl, mm[n:, n:], 0.0).astype(BF16)], axis=1)
            e_last_col = jnp.sum(jnp.where(eye, e_last[:, cs], 0.0), axis=-1, keepdims=True)
            units.append(dict(rs=rs, cs=cs, grp=grp, ar=ar, bkh=bkh, v_bd=v_bd, ab=ab, rbk=rbk,
                              akv=_dot(ak, v_bd), e_last_col=e_last_col,
                              tinv=eye_f + jnp.where(level_masks[0], ab, 0.0)))
            yield
    for l in range(1, n_levels):
        for un in units:
            t_bf = un["tinv"].astype(BF16)
            join = jnp.where(level_masks[l], un["ab"], 0.0).astype(BF16)
            un["tinv"] = un["tinv"] + _dot(t_bf, _dot(join, t_bf).astype(BF16))
            yield
    for un in units:
        h0 = s_ref[un["grp"]]
        ah = _dot(un["ar"], h0.astype(BF16))
        u = _dot(un["tinv"].astype(BF16), (ah[:n, :] + un["akv"]).astype(BF16))
        uv = jnp.concatenate([u.astype(BF16), un["v_bd"]], axis=0)
        o = ah[n:, :] + _dot(un["rbk"], uv)
        y_ref[un["rs"], un["cs"]] = o[0:CHUNK, :] + o[CHUNK:, :]
        s_ref[un["grp"]] = h0 * un["e_last_col"] + jnp.where(same_head, _dot_tn(un["bkh"], uv), 0.0)
        yield

    y = y_ref[...]
    inv_d = 1.0 / HEAD_DIM
    mean = _dot_f32_lhs(y, head_ones) * inv_d
    yc = y - mean
    var = _dot_f32_lhs(yc * yc, head_ones) * inv_d
    yn = yc * lax.rsqrt(var + R_LN_EPS) * lng_ref[...] + lnb_ref[...]
    bonus = _dot_f32_lhs(r * k2 * rk_ref[...], head_ones)
    o_ref[...] = ((yn + bonus * v) * gate).astype(o_ref.dtype)


def _rwkv_parts(proj, mu, w0, w_up, a0, a_up, g_up, k_k, k_a, r_k, ln_g, ln_b, batch, seq, tb):
    nblk = seq // tb
    w_lora = jnp.zeros((128, 3 * R_WIDTH), F32)
    w_lora = w_lora.at[0:32, 0:R_WIDTH].set(w_up)
    w_lora = w_lora.at[32:64, R_WIDTH:2 * R_WIDTH].set(a_up)
    w_lora = w_lora.at[64:128, 2 * R_WIDTH:].set(g_up)
    const = lambda b, i: (0, 0)
    vec = pl.BlockSpec((1, R_WIDTH), const)
    return dict(
        args=(proj, mu[None, :], w_lora, w0[None, :], a0[None, :], k_k[None, :], k_a[None, :],
              r_k.reshape(1, R_WIDTH), ln_g[None, :], ln_b[None, :]),
        in_specs=[
            pl.BlockSpec((tb, R_PROJ), lambda b, i: (b * nblk + i, 0)),
            pl.BlockSpec((1, R_PROJ), const),
            pl.BlockSpec((128, 3 * R_WIDTH), const),
            vec, vec, vec, vec, vec, vec, vec,
        ],
        out_spec=pl.BlockSpec((tb, R_WIDTH), lambda b, i: (b * nblk + i, 0)),
        out_shape=jax.ShapeDtypeStruct((batch * seq, R_WIDTH), BF16),
        scratch=[
            pltpu.VMEM((8, R_PROJ), F32),
            pltpu.VMEM((N_HEADS // RWKV_GROUP_HEADS, RWKV_GROUP_HEADS * HEAD_DIM, RWKV_GROUP_HEADS * HEAD_DIM), F32),
            pltpu.VMEM((tb, R_WIDTH), F32),
        ])


def _recurrent_mixers(mlstm_args, rwkv_args, batch, seq, tb=512):
    mp = _mlstm_parts(*mlstm_args, batch, seq, tb)
    rp = _rwkv_parts(*rwkv_args, batch, seq, tb)
    n_m_in, n_r_in = len(mp["args"]), len(rp["args"])
    n_m_scr = len(mp["scratch"])

    def kern(*refs):
        m_in = refs[:n_m_in]
        r_in = refs[n_m_in:n_m_in + n_r_in]
        om_ref, or_ref = refs[n_m_in + n_r_in:n_m_in + n_r_in + 2]
        scr = refs[n_m_in + n_r_in + 2:]
        rwkv = _rwkv_kernel(*r_in, or_ref, *scr[n_m_scr:], tb=tb)
        mlstm = _mlstm_kernel(*m_in, om_ref, *scr[:n_m_scr], tb=tb)
        for _ in range(RWKV_HEAD_START):
            next(rwkv)
        live = [rwkv, mlstm]
        while live:
            for gen, turns in ((mlstm, 1), (rwkv, RWKV_TURNS_PER_MLSTM_TURN)):
                for _ in range(turns):
                    if gen in live and next(gen, "done") == "done":
                        live.remove(gen)

    return pl.pallas_call(
        kern,
        grid=(batch, seq // tb),
        in_specs=mp["in_specs"] + rp["in_specs"],
        out_specs=[mp["out_spec"], rp["out_spec"]],
        out_shape=[mp["out_shape"], rp["out_shape"]],
        scratch_shapes=mp["scratch"] + rp["scratch"],
        compiler_params=pltpu.CompilerParams(
            dimension_semantics=("parallel", "arbitrary"), vmem_limit_bytes=VMEM_LIMIT),
        name="recurrent_mixers",
    )(*mp["args"], *rp["args"])


def _attn_kernel(lq1_ref, lk1_ref, lq2_ref, lk2_ref, g_ref, q_ref, k_ref, v_ref, o_ref,
                 vt_ref, sa_ref, sb_ref, p_ref, m_ref, acc_ref, *, tq, lam_init):
    qi = pl.program_id(2)
    dv = 2 * HEAD_DIM
    heads = range(ATTN_HEADS_PER_STEP)
    streams = range(2 * ATTN_HEADS_PER_STEP)

    @pl.when(qi == 0)
    def _():
        for hh in heads:
            for jj in range(vt_ref.shape[1]):
                v_blk = v_ref[jj * tq:(jj + 1) * tq, hh * dv:(hh + 1) * dv]
                vt_ref[hh, jj, :dv, :] = v_blk.astype(F32).T.astype(BF16)
                vt_ref[hh, jj, dv:, :] = jnp.ones((ATTN_SUM_ROWS, tq), BF16)

    lam = (jnp.exp(jnp.sum(lq1_ref[...] * lk1_ref[...], axis=-1, keepdims=True))
           - jnp.exp(jnp.sum(lq2_ref[...] * lk2_ref[...], axis=-1, keepdims=True)) + lam_init)
    lane = _iota2((tq, dv), 1)
    q_halves = []
    for hh in heads:
        q = q_ref[:, hh * dv:(hh + 1) * dv]
        zero = jnp.zeros_like(q)
        q_halves += [jnp.where(lane < HEAD_DIM, q, zero), jnp.where(lane >= HEAD_DIM, q, zero)]
    strips = [slice(r * ATTN_STRIP, (r + 1) * ATTN_STRIP) for r in range(tq // ATTN_STRIP)]
    groups = ATTN_STRIP // 8

    def put_scores(j, dst_ref):
        start = pl.multiple_of(j * tq, tq)
        for hh in heads:
            kb = k_ref[pl.ds(start, tq), hh * dv:(hh + 1) * dv]
            for u in (2 * hh, 2 * hh + 1):
                dst_ref[u] = _dot_nt(kb, q_halves[u])

    def strip_scores(src_ref, i, rows, masked):
        s = src_ref[i, rows, :]
        if masked:
            keep = _iota2((ATTN_STRIP, tq), 0) + rows.start <= _iota2((ATTN_STRIP, tq), 1)
            s = jnp.where(keep, s, -jnp.inf)
        return s.reshape(groups, 8, tq)

    def update(j, src_ref, masked):
        for i in streams:
            m = m_ref[i]
            mx8 = None
            for rows in strips:
                t = jnp.max(strip_scores(src_ref, i, rows, masked), axis=0)
                mx8 = t if mx8 is None else jnp.maximum(mx8, t)
            m_new = jnp.maximum(m, jnp.max(mx8, axis=0, keepdims=True))
            for rows in strips:
                pr = jnp.exp(strip_scores(src_ref, i, rows, masked) - m_new)
                p_ref[i, rows, :] = pr.reshape(ATTN_STRIP, tq).astype(BF16)
            m_ref[i] = m_new
            acc_ref[i] = jnp.exp(m - m_new) * acc_ref[i] + _dot(vt_ref[i // 2, j], p_ref[i])

    m_ref[...] = jnp.full(m_ref.shape, -1e30, F32)
    acc_ref[...] = jnp.zeros_like(acc_ref)

    def body(t, carry):
        j = 2 * t
        put_scores(j + 1, sb_ref)
        update(j, sa_ref, False)
        put_scores(j + 2, sa_ref)
        update(j + 1, sb_ref, False)
        return carry

    put_scores(0, sa_ref)
    lax.fori_loop(0, qi // 2, body, 0)

    @pl.when(qi % 2 == 1)
    def _():
        put_scores(qi, sb_ref)
        update(qi - 1, sa_ref, False)
        update(qi, sb_ref, True)

    @pl.when(qi % 2 == 0)
    def _():
        update(qi, sa_ref, True)

    for hh in heads:
        a1, a2 = acc_ref[2 * hh], acc_ref[2 * hh + 1]
        o = (a1[:dv, :] / a1[dv:dv + 1, :] - lam * (a2[:dv, :] / a2[dv:dv + 1, :])).T
        o = o * lax.rsqrt(jnp.mean(o * o, axis=-1, keepdims=True) + A_NORM_EPS) * g_ref[:, hh * dv:(hh + 1) * dv]
        o_ref[:, hh * dv:(hh + 1) * dv] = (o * (1.0 - lam_init)).astype(o_ref.dtype)


def _diff_attn(qkv, lq1, lk1, lq2, lk2, norm_g, lam_init, batch, seq, tq=512):
    nq = seq // tq
    nh = ATTN_HEADS_PER_STEP
    n_groups = N_HEADS // nh
    wide = nh * 2 * HEAD_DIM
    kern = functools.partial(_attn_kernel, tq=tq, lam_init=lam_init)
    const = lambda b, g, i: (0, 0)
    lvec = pl.BlockSpec((1, HEAD_DIM), const)
    return pl.pallas_call(
        kern,
        grid=(batch, n_groups, nq),
        in_specs=[
            lvec, lvec, lvec, lvec,
            pl.BlockSpec((1, wide), lambda b, g, i: (0, g)),
            pl.BlockSpec((tq, wide), lambda b, g, i: (b * nq + i, g)),
            pl.BlockSpec((seq, wide), lambda b, g, i: (b, n_groups + g)),
            pl.BlockSpec((seq, wide), lambda b, g, i: (b, 2 * n_groups + g)),
        ],
        out_specs=pl.BlockSpec((tq, wide), lambda b, g, i: (b * nq + i, g)),
        out_shape=jax.ShapeDtypeStruct((batch * seq, A_WIDTH), BF16),
        scratch_shapes=[
            pltpu.VMEM((nh, nq, 2 * HEAD_DIM + ATTN_SUM_ROWS, tq), BF16),
            pltpu.VMEM((2 * nh, tq, tq), F32),
            pltpu.VMEM((2 * nh, tq, tq), F32),
            pltpu.VMEM((2 * nh, tq, tq), BF16),
            pltpu.VMEM((2 * nh, 1, tq), F32),
            pltpu.VMEM((2 * nh, 2 * HEAD_DIM + ATTN_SUM_ROWS, tq), F32),
        ],
        compiler_params=pltpu.CompilerParams(
            dimension_semantics=("parallel", "parallel", "arbitrary"), vmem_limit_bytes=VMEM_LIMIT),
        name="diff_attn",
    )(lq1[None, :], lk1[None, :], lq2[None, :], lk2[None, :], norm_g[None, :], qkv, qkv, qkv)


def _out_ffn_kernel(x_ref, ym_ref, yr_ref, ya_ref, wo_ref, g2_ref, wu_ref, wd_ref, gf_ref, o_ref,
                    h_ref, acc_ref, *, final_norm):
    j = pl.program_id(1)

    @pl.when(j == 0)
    def _():
        mix = jnp.concatenate([ym_ref[...], yr_ref[...], ya_ref[...]], axis=1)
        xn = x_ref[...] + _dot(mix, wo_ref[...])
        acc_ref[...] = xn
        hn = xn * lax.rsqrt(jnp.mean(xn * xn, axis=-1, keepdims=True) + NORM_EPS) * g2_ref[...]
        h_ref[...] = hn.astype(BF16)

    u = jnp.maximum(_dot(h_ref[...], wu_ref[...]), 0.0)
    acc_ref[...] += _dot((u * u).astype(BF16), wd_ref[...])

    @pl.when(j == pl.num_programs(1) - 1)
    def _():
        y = acc_ref[...]
        if final_norm:
            y = y * lax.rsqrt(jnp.mean(y * y, axis=-1, keepdims=True) + NORM_EPS) * gf_ref[...]
        o_ref[...] = y


def _out_ffn(x, ym, yr, ya, w_out, g2, w_up, w_down, g_final, final_norm, tm=1024, tf=1024):
    t, d = x.shape
    f = w_up.shape[1]
    kern = functools.partial(_out_ffn_kernel, final_norm=final_norm)
    rows = lambda i, j: (i, 0)
    const = lambda i, j: (0, 0)
    return pl.pallas_call(
        kern,
        grid=(t // tm, f // tf),
        in_specs=[
            pl.BlockSpec((tm, d), rows),
            pl.BlockSpec((tm, M_WIDTH), rows),
            pl.BlockSpec((tm, R_WIDTH), rows),
            pl.BlockSpec((tm, A_WIDTH), rows),
            pl.BlockSpec((d, d), const),
            pl.BlockSpec((1, d), const),
            pl.BlockSpec((d, tf), lambda i, j: (0, j)),
            pl.BlockSpec((tf, d), lambda i, j: (j, 0)),
            pl.BlockSpec((1, d), const),
        ],
        out_specs=pl.BlockSpec((tm, d), rows),
        out_shape=jax.ShapeDtypeStruct((t, d), F32),
        scratch_shapes=[pltpu.VMEM((tm, d), BF16), pltpu.VMEM((tm, d), F32)],
        compiler_params=pltpu.CompilerParams(
            dimension_semantics=("parallel", "arbitrary"), vmem_limit_bytes=VMEM_LIMIT),
        name="out_ffn",
    )(x, ym, yr, ya, w_out, g2, w_up, w_down, g_final)


def _split_in_proj(w):
    m_main = w[:, 0:4 * M_WIDTH]
    m_gate = w[:, 4 * M_WIDTH:4 * M_WIDTH + 2 * N_HEADS]
    r0 = 4 * M_WIDTH + 2 * N_HEADS
    r_all = w[:, r0:r0 + R_PROJ]
    a0 = r0 + R_PROJ
    a_q = w[:, a0:a0 + A_WIDTH] * (HEAD_DIM ** -0.5)
    a_kv = w[:, a0 + A_WIDTH:a0 + A_PROJ]
    pad = jnp.zeros((w.shape[0], 128 - 2 * N_HEADS), w.dtype)
    w_main = jnp.concatenate([r_all, m_gate, pad, m_main], axis=1).astype(BF16)
    w_attn = jnp.concatenate([a_q, a_kv], axis=1).astype(BF16)
    return w_main, w_attn, m_gate.T.astype(BF16)


def kernel(x, norm1_g, w_in, m_conv_w, m_conv_b, m_b_i, m_b_f, m_norm_g, r_mu, r_w0, r_w_up, r_a0, r_a_up, r_g_up, r_k_k, r_k_a, r_r_k, r_ln_g, r_ln_b, a_lq1, a_lk1, a_lq2, a_lk2, a_norm_g, w_out, norm2_g, w_ff_up, w_ff_down, final_g):
    batch, seq, d = x.shape
    depth = w_in.shape[0]
    xt = x.reshape(batch * seq, d)
    for l in range(depth):
        w_main, w_attn, w_gate_t = _split_in_proj(w_in[l])
        g1 = norm1_g[l][None, :]
        proj, qkv, gates_t = _in_proj(xt, g1, w_main, w_attn, w_gate_t)
        gates_r = (gates_t.reshape(2 * N_HEADS, batch * seq // CHUNK, CHUNK).transpose(1, 0, 2)
                   .reshape(batch * seq // CHUNK, 2 * N_HEADS * CHUNK))
        y_m, y_r = _recurrent_mixers(
            (proj, gates_r, m_conv_w[l], m_conv_b[l], m_b_i[l], m_b_f[l], m_norm_g[l]),
            (proj, r_mu[l], r_w0[l], r_w_up[l], r_a0[l], r_a_up[l], r_g_up[l],
             r_k_k[l], r_k_a[l], r_r_k[l], r_ln_g[l], r_ln_b[l]),
            batch, seq)
        lam_init = 0.8 - 0.6 * math.exp(-0.3 * l)
        y_a = _diff_attn(qkv, a_lq1[l], a_lk1[l], a_lq2[l], a_lk2[l], a_norm_g[l], lam_init, batch, seq)
        xt = _out_ffn(xt, y_m, y_r, y_a, w_out[l].astype(BF16), norm2_g[l][None, :],
                      w_ff_up[l].astype(BF16), w_ff_down[l].astype(BF16), final_g[None, :],
                      final_norm=(l == depth - 1))
    return xt.reshape(batch, seq, d)
```
